```python
import jax
import jax.numpy as jnp
from jax import lax
import numpy as np

D_MODEL = 1024
BATCH = 2
SEQ = 8192
DEPTH = 1

HEAD_DIM = 64
N_HEADS = D_MODEL // HEAD_DIM
N_HEADS_SB = N_HEADS // 2
N_HEADS_DIL = N_HEADS - N_HEADS_SB
D_SB = N_HEADS_SB * HEAD_DIM
D_DIL = N_HEADS_DIL * HEAD_DIM
D_IN = 3 * D_SB + 3 * D_DIL
D_FF = 2816
DILATED_PATTERNS = ((128, 1), (512, 4), (2048, 16))
BLOCK = 128
ROPE_THETA = 10000.0
RMS_EPS = 1e-6
HALF_STEP = 0.5

kernel_name = 'hybrid_stickbreak_dilated_macaron'


def rmsnorm(x, gain):
    xf = x.astype(jnp.float32)
    y = xf * lax.rsqrt(jnp.mean(xf * xf, axis=-1, keepdims=True) + RMS_EPS)
    return (y * gain.astype(jnp.float32)).astype(x.dtype)


def swiglu(x, w_gate, w_up, w_down):
    return (jax.nn.silu(x @ w_gate) * (x @ w_up)) @ w_down


def rotary(t, positions):
    half = t.shape[-1] // 2
    inv_freq = ROPE_THETA ** (-jnp.arange(half, dtype=jnp.float32) / half)
    ang = positions.astype(jnp.float32)[:, None] * inv_freq[None, :]
    cos, sin = jnp.cos(ang), jnp.sin(ang)
    tf = t.astype(jnp.float32)
    t1, t2 = tf[..., :half], tf[..., half:]
    return jnp.concatenate([t1 * cos - t2 * sin, t2 * cos + t1 * sin], axis=-1).astype(t.dtype)


def split_heads(t, n_heads):
    b, s, _ = t.shape
    return t.reshape(b, s, n_heads, HEAD_DIM).transpose(0, 2, 1, 3)


def merge_heads(t):
    b, h, s, d = t.shape
    return t.transpose(0, 2, 1, 3).reshape(b, s, h * d)


def stick_breaking_attention(q, k, v):
    b, h, s, d = q.shape
    nb = s // BLOCK
    scale = d ** -0.5
    qb = q.reshape(b, h, nb, BLOCK, d).transpose(2, 0, 1, 3, 4)
    key_pos = jnp.arange(s)

    def one_block(args):
        q_blk, blk = args
        q_pos = blk * BLOCK + jnp.arange(BLOCK)
        z = jnp.einsum('bhqd,bhkd->bhqk', q_blk, k, preferred_element_type=jnp.float32) * scale
        mask = key_pos[None, :] < q_pos[:, None]
        log_beta = jax.nn.log_sigmoid(z)
        log_stay = jnp.where(mask, jax.nn.log_sigmoid(-z), 0.0)
        later = lax.cumsum(log_stay, axis=3, reverse=True) - log_stay
        weights = jnp.where(mask, jnp.exp(log_beta + later), 0.0)
        return jnp.einsum('bhqk,bhkd->bhqd', weights.astype(v.dtype), v)

    out = lax.map(one_block, (qb, jnp.arange(nb)))
    return out.transpose(1, 2, 0, 3, 4).reshape(b, h, s, d)


def _dilated_pattern(q, k, v, window, dilation):
    b, h, s, d = q.shape
    span = window // dilation
    n_comp = s // dilation
    nb = n_comp // BLOCK

    def to_blocks(t):
        t = t.reshape(b, h, n_comp, dilation, d).transpose(0, 1, 3, 2, 4)
        return t.reshape(b, h, dilation, nb, BLOCK, d)

    def with_previous(t):
        prev = jnp.pad(t, ((0, 0), (0, 0), (0, 0), (1, 0), (0, 0), (0, 0)))[:, :, :, :-1]
        return jnp.concatenate([prev, t], axis=4)

    qb = to_blocks(q)
    kb = with_previous(to_blocks(k))
    vb = with_previous(to_blocks(v))
    z = jnp.einsum('bhcnqd,bhcnkd->bhcnqk', qb, kb, preferred_element_type=jnp.float32) * (d ** -0.5)
    q_idx = jnp.arange(BLOCK)[:, None] + BLOCK
    k_idx = jnp.arange(2 * BLOCK)[None, :]
    dist = q_idx - k_idx
    band = (dist >= 0) & (dist <= span)
    has_prev = (jnp.arange(nb) > 0)[:, None, None] | (k_idx >= BLOCK)[None]
    valid = band[None] & has_prev
    z = jnp.where(valid, z, -jnp.inf)
    m = jnp.max(z, axis=-1, keepdims=True)
    p = jnp.exp(z - m)
    denom = jnp.sum(p, axis=-1, keepdims=True)
    o = jnp.einsum('bhcnqk,bhcnkd->bhcnqd', p, vb.astype(jnp.float32)) / denom
    lse = m + jnp.log(denom)

    def from_blocks(t):
        e = t.shape[-1]
        t = t.reshape(b, h, dilation, n_comp, e).transpose(0, 1, 3, 2, 4)
        return t.reshape(b, h, s, e)

    return from_blocks(o), from_blocks(lse)


def dilated_mixture_attention(q, k, v):
    s = q.shape[2]
    outs, lses = [], []
    for window, dilation in DILATED_PATTERNS:
        unit = BLOCK * dilation
        s_pad = -(-s // unit) * unit
        pad = ((0, 0), (0, 0), (0, s_pad - s), (0, 0))
        o, lse = _dilated_pattern(jnp.pad(q, pad), jnp.pad(k, pad), jnp.pad(v, pad), window, dilation)
        outs.append(o[:, :, :s])
        lses.append(lse[:, :, :s])
    alpha = jax.nn.softmax(jnp.stack(lses), axis=0)
    return jnp.sum(alpha * jnp.stack(outs), axis=0).astype(q.dtype)


def token_mixer(h, w_in, sb_out_norm, dil_out_norm, w_out):
    seq = h.shape[1]
    proj = h @ w_in
    cuts = [D_SB, 2 * D_SB, 3 * D_SB, 3 * D_SB + D_DIL, 3 * D_SB + 2 * D_DIL]
    q_sb, k_sb, v_sb, q_dl, k_dl, v_dl = jnp.split(proj, cuts, axis=-1)
    positions = jnp.arange(seq)
    o_sb = stick_breaking_attention(split_heads(q_sb, N_HEADS_SB), split_heads(k_sb, N_HEADS_SB),
                                    split_heads(v_sb, N_HEADS_SB))
    o_dl = dilated_mixture_attention(rotary(split_heads(q_dl, N_HEADS_DIL), positions),
                                     rotary(split_heads(k_dl, N_HEADS_DIL), positions),
                                     split_heads(v_dl, N_HEADS_DIL))
    merged = jnp.concatenate([rmsnorm(merge_heads(o_sb), sb_out_norm),
                              rmsnorm(merge_heads(o_dl), dil_out_norm)], axis=-1)
    return merged @ w_out


def setup_inputs(seed: int = 0) -> dict:
    key = jax.random.key(seed)
    ks = jax.random.split(key, 16)

    def normal(k, shape, scale):
        return jax.random.normal(k, shape, jnp.float32) * scale

    def gain(k, shape):
        return 1.0 + 0.02 * jax.random.normal(k, shape, jnp.float32)

    dm, df = D_MODEL ** -0.5, D_FF ** -0.5
    return {
        'x': normal(ks[0], (BATCH, SEQ, D_MODEL), 1.0),
        'ffn1_norm': gain(ks[1], (DEPTH, D_MODEL)),
        'ffn1_w_gate': normal(ks[2], (DEPTH, D_MODEL, D_FF), dm),
        'ffn1_w_up': normal(ks[3], (DEPTH, D_MODEL, D_FF), dm),
        'ffn1_w_down': normal(ks[4], (DEPTH, D_FF, D_MODEL), df),
        'mix_norm': gain(ks[5], (DEPTH, D_MODEL)),
        'w_in': normal(ks[6], (DEPTH, D_MODEL, D_IN), dm),
        'sb_out_norm': gain(ks[7], (DEPTH, D_SB)),
        'dil_out_norm': gain(ks[8], (DEPTH, D_DIL)),
        'w_out': normal(ks[9], (DEPTH, D_MODEL, D_MODEL), dm),
        'ffn2_norm': gain(ks[10], (DEPTH, D_MODEL)),
        'ffn2_w_gate': normal(ks[11], (DEPTH, D_MODEL, D_FF), dm),
        'ffn2_w_up': normal(ks[12], (DEPTH, D_MODEL, D_FF), dm),
        'ffn2_w_down': normal(ks[13], (DEPTH, D_FF, D_MODEL), df),
        'final_norm': gain(ks[14], (D_MODEL,)),
    }


def reference(x, ffn1_norm, ffn1_w_gate, ffn1_w_up, ffn1_w_down, mix_norm, w_in, sb_out_norm,
              dil_out_norm, w_out, ffn2_norm, ffn2_w_gate, ffn2_w_up, ffn2_w_down, final_norm):
    for layer in range(DEPTH):
        x = x + HALF_STEP * swiglu(rmsnorm(x, ffn1_norm[layer]), ffn1_w_gate[layer],
                                   ffn1_w_up[layer], ffn1_w_down[layer])
        x = x + token_mixer(rmsnorm(x, mix_norm[layer]), w_in[layer], sb_out_norm[layer],
                            dil_out_norm[layer], w_out[layer])
        x = x + HALF_STEP * swiglu(rmsnorm(x, ffn2_norm[layer]), ffn2_w_gate[layer],
                                   ffn2_w_up[layer], ffn2_w_down[layer])
    return rmsnorm(x, final_norm)
```

```python
import functools
import math

import jax
import jax.numpy as jnp
from jax import lax
from jax.experimental import pallas as pl
from jax.experimental.pallas import tpu as pltpu

D_MODEL = 1024
HEAD_DIM = 64
D_GROUP = 512
D_IN = 6 * D_GROUP
D_FF = 2816
DILATED_PATTERNS = ((128, 1), (512, 4), (2048, 16))
ROPE_THETA = 10000.0
RMS_EPS = 1e-6
HALF_STEP = 0.5

LANES = 128
LOG2E = math.log2(math.e)
INV_LN2 = 1.0 / math.log(2.0)
Q_SCALE = HEAD_DIM ** -0.5 * LOG2E

TM = 512
FF_CHUNK = 256
SB_TQ = 256
SB_KB = 256
DL_TQ = 128
VMEM_LIMIT = 48 * 1024 * 1024

F32 = jnp.float32
BF16 = jnp.bfloat16


def _rms(x):
    return x * lax.rsqrt(jnp.mean(x * x, axis=-1, keepdims=True) + RMS_EPS)


def _params(*sem):
    return pltpu.CompilerParams(dimension_semantics=sem, vmem_limit_bytes=VMEM_LIMIT)


def _resident(shape):
    return pl.BlockSpec(shape, lambda *_: (0,) * len(shape), pipeline_mode=pl.Buffered(1))


def _ffn_kernel(x_ref, gain_ref, wg_ref, wu_ref, wd_ref, *rest, final_norm):
    o_ref = rest[-1]
    x = x_ref[...]
    h = (_rms(x) * gain_ref[...]).astype(BF16)
    acc = jnp.zeros(x.shape, F32)
    for c in range(0, D_FF, FF_CHUNK):
        g = jnp.dot(h, wg_ref[:, c:c + FF_CHUNK], preferred_element_type=F32)
        u = jnp.dot(h, wu_ref[:, c:c + FF_CHUNK], preferred_element_type=F32)
        a = (g * jax.nn.sigmoid(g) * u).astype(BF16)
        acc = acc + jnp.dot(a, wd_ref[c:c + FF_CHUNK, :], preferred_element_type=F32)
    y = x + HALF_STEP * acc
    if final_norm:
        y = _rms(y) * rest[0][...]
    o_ref[...] = y


def _ffn_block(x, gain, wg, wu, wd, final_gain=None):
    t = x.shape[0]
    row = pl.BlockSpec((TM, D_MODEL), lambda i: (i, 0))
    vec = _resident((1, D_MODEL))
    in_specs = [row, vec, _resident((D_MODEL, D_FF)), _resident((D_MODEL, D_FF)), _resident((D_FF, D_MODEL))]
    args = [x, gain.reshape(1, D_MODEL), wg, wu, wd]
    if final_gain is not None:
        in_specs.append(vec)
        args.append(final_gain.reshape(1, D_MODEL))
    return pl.pallas_call(
        functools.partial(_ffn_kernel, final_norm=final_gain is not None),
        grid=(t // TM,),
        in_specs=in_specs,
        out_specs=row,
        out_shape=jax.ShapeDtypeStruct((t, D_MODEL), F32),
        compiler_params=_params("parallel"),
        name="ffn_block",
    )(*args)


def _in_proj_kernel(x_ref, gain_ref, w_ref, cos_ref, sin_ref,
                    qsb_ref, ktsb_ref, vsb_ref, qdl_ref, kdl_ref, vdl_ref):
    h = (_rms(x_ref[...]) * gain_ref[...]).astype(BF16)

    def proj(j):
        return jnp.dot(h, w_ref[:, j * D_GROUP:(j + 1) * D_GROUP], preferred_element_type=F32)

    cos = cos_ref[...]
    sin = sin_ref[...]
    lane = lax.broadcasted_iota(jnp.int32, cos.shape, 1)
    low_half = (lane % HEAD_DIM) < HEAD_DIM // 2

    def rope(t):
        outs = []
        for j in range(D_GROUP // LANES):
            tj = t[:, j * LANES:(j + 1) * LANES]
            ahead = pltpu.roll(tj, LANES - HEAD_DIM // 2, 1)
            behind = pltpu.roll(tj, HEAD_DIM // 2, 1)
            outs.append(tj * cos + jnp.where(low_half, ahead, behind) * sin)
        return jnp.concatenate(outs, axis=1)

    qsb_ref[...] = (proj(0) * Q_SCALE).astype(BF16)
    kt = proj(1).T
    for hp in range(D_GROUP // LANES):
        for kb in range(TM // SB_KB):
            ktsb_ref[hp, kb] = kt[hp * LANES:(hp + 1) * LANES, kb * SB_KB:(kb + 1) * SB_KB].astype(BF16)
    vsb_ref[...] = proj(2).astype(BF16)
    qdl_ref[...] = (rope(proj(3)) * Q_SCALE).astype(BF16)
    kdl_ref[...] = rope(proj(4)).astype(BF16)
    vdl_ref[...] = proj(5).astype(BF16)


def _in_proj(x, gain, w_in, cos, sin, batch, seq):
    t = x.shape[0]
    n_seq_tiles = seq // TM
    row = lambda w: pl.BlockSpec((TM, w), lambda b, i: (b * n_seq_tiles + i, 0))
    grp = jax.ShapeDtypeStruct((t, D_GROUP), BF16)
    kt_shape = (batch, D_GROUP // LANES, seq // SB_KB, LANES, SB_KB)
    return pl.pallas_call(
        _in_proj_kernel,
        grid=(batch, n_seq_tiles),
        in_specs=[row(D_MODEL), _resident((1, D_MODEL)), _resident((D_MODEL, D_IN)),
                  pl.BlockSpec((TM, LANES), lambda b, i: (i, 0)),
                  pl.BlockSpec((TM, LANES), lambda b, i: (i, 0))],
        out_specs=[row(D_GROUP),
                   pl.BlockSpec((None, D_GROUP // LANES, TM // SB_KB, LANES, SB_KB),
                                lambda b, i: (b, 0, i, 0, 0)),
                   row(D_GROUP), row(D_GROUP), row(D_GROUP), row(D_GROUP)],
        out_shape=[grp, jax.ShapeDtypeStruct(kt_shape, BF16), grp, grp, grp, grp],
        compiler_params=_params("parallel", "parallel"),
        name="in_proj",
    )(x, gain.reshape(1, D_MODEL), w_in, cos, sin)


def _sb_kernel(q_ref, kt_ref, v_ref, csum_ref, o_ref, acc_ref, run_ref):
    i = pl.program_id(2)
    q = q_ref[...]
    lane = lax.broadcasted_iota(jnp.int32, (SB_TQ, LANES), 1)
    row = lax.broadcasted_iota(jnp.int32, (SB_TQ, SB_KB), 0)
    col = lax.broadcasted_iota(jnp.int32, (SB_TQ, SB_KB), 1)
    strictly_before = col < row
    csum_w = csum_ref[...]

    def neg_cumsum(p_half):
        hi = p_half.astype(BF16)
        lo = (p_half - hi.astype(F32)).astype(BF16)
        return jnp.dot(jnp.concatenate([hi, lo], axis=1), csum_w, preferred_element_type=F32)

    outs = []
    for hh in range(LANES // HEAD_DIM):
        in_head = (lane >= hh * HEAD_DIM) & (lane < (hh + 1) * HEAD_DIM)
        qh = jnp.where(in_head, q, jnp.zeros_like(q))

        def block(kb, diagonal):
            v = v_ref[pl.ds(pl.multiple_of(kb * SB_KB, SB_KB), SB_KB), :]
            w = jnp.dot(qh, kt_ref[kb], preferred_element_type=F32)
            p = jnp.maximum(w, 0.0) + jnp.log(1.0 + jnp.exp2(-jnp.abs(w))) * INV_LN2
            if diagonal:
                p = jnp.where(strictly_before, p, 0.0)
            run = run_ref[...]
            c_new = neg_cumsum(p[:, LANES:])
            c_old = neg_cumsum(p[:, :LANES])
            run_mid = run + c_new[:, LANES:]
            x_new = w[:, LANES:] + c_new[:, :LANES] + run
            x_old = w[:, :LANES] + c_old[:, :LANES] + run_mid
            run_ref[...] = run_mid + c_old[:, LANES:]
            a = jnp.exp2(jnp.concatenate([x_old, x_new], axis=1))
            if diagonal:
                a = jnp.where(strictly_before, a, 0.0)
            acc_ref[...] += jnp.dot(a.astype(BF16), v, preferred_element_type=F32)

        run_ref[...] = jnp.zeros_like(run_ref)
        acc_ref[...] = jnp.zeros_like(acc_ref)
        block(i, True)

        def body(j, carry):
            block(i - 1 - j, False)
            return carry

        lax.fori_loop(0, i, body, 0)
        outs.append(acc_ref[...])
    o_ref[...] = jnp.where(lane < HEAD_DIM, outs[0], outs[1])


def _csum_weights():
    j = jnp.arange(LANES)[:, None]
    s = jnp.arange(LANES)[None, :]
    tri = jnp.where(j >= s, -1.0, 0.0)
    half = jnp.concatenate([tri, jnp.full((LANES, LANES), -1.0)], axis=1)
    return jnp.concatenate([half, half], axis=0).astype(BF16)


def _sb_attention(q, kt, v, batch, seq):
    t = q.shape[0]
    n_q = seq // SB_TQ
    n_hp = D_GROUP // LANES
    return pl.pallas_call(
        _sb_kernel,
        grid=(batch, n_hp, n_q),
        in_specs=[pl.BlockSpec((SB_TQ, LANES), lambda b, hp, i: (b * n_q + i, hp)),
                  pl.BlockSpec((None, None, seq // SB_KB, LANES, SB_KB), lambda b, hp, i: (b, hp, 0, 0, 0)),
                  pl.BlockSpec((seq, LANES), lambda b, hp, i: (b, hp)),
                  _resident((2 * LANES, 2 * LANES))],
        out_specs=pl.BlockSpec((SB_TQ, LANES), lambda b, hp, i: (b * n_q + i, hp)),
        out_shape=jax.ShapeDtypeStruct((t, D_GROUP), F32),
        scratch_shapes=[pltpu.VMEM((SB_TQ, LANES), F32), pltpu.VMEM((SB_TQ, LANES), F32)],
        compiler_params=_params("parallel", "parallel", "arbitrary"),
        name="sb_attention",
    )(q, kt, v, _csum_weights())


def _dilated_kernel(q_ref, k_ref, v_ref, o_ref, lse_ref, *, span):
    n_tiles = q_ref.shape[0] // DL_TQ
    lane = lax.broadcasted_iota(jnp.int32, (DL_TQ, LANES), 1)
    row = lax.broadcasted_iota(jnp.int32, (DL_TQ, 2 * DL_TQ), 0)
    col = lax.broadcasted_iota(jnp.int32, (DL_TQ, 2 * DL_TQ), 1)

    def tile(i, carry):
        q0 = pl.multiple_of(i * DL_TQ, DL_TQ)
        k0 = pl.multiple_of(jnp.maximum(i - 1, 0) * DL_TQ, DL_TQ)
        q = q_ref[pl.ds(q0, DL_TQ), :]
        k = k_ref[pl.ds(k0, 2 * DL_TQ), :]
        v = v_ref[pl.ds(k0, 2 * DL_TQ), :]
        dist = (q0 - k0) + row - col
        valid = (dist >= 0) & (dist <= span)
        outs, lses = [], []
        for hh in range(LANES // HEAD_DIM):
            in_head = (lane >= hh * HEAD_DIM) & (lane < (hh + 1) * HEAD_DIM)
            qh = jnp.where(in_head, q, jnp.zeros_like(q))
            z = lax.dot_general(qh, k, (((1,), (1,)), ((), ())), preferred_element_type=F32)
            z = jnp.where(valid, z, -jnp.inf)
            m = jnp.max(z, axis=-1, keepdims=True)
            p = jnp.exp2(z - m)
            denom = jnp.sum(p, axis=-1, keepdims=True)
            o = jnp.dot(p.astype(BF16), v, preferred_element_type=F32) / denom
            outs.append(o)
            lses.append(jnp.broadcast_to(m + jnp.log(denom) * INV_LN2, (DL_TQ, LANES)))
        o_ref[pl.ds(q0, DL_TQ), :] = jnp.where(lane < HEAD_DIM, outs[0], outs[1])
        lse_ref[pl.ds(q0, DL_TQ), :] = jnp.where(lane < HEAD_DIM, lses[0], lses[1])
        return carry

    lax.fori_loop(0, n_tiles, tile, 0)


def _dilated_pattern(q, k, v, batch, seq, window, dilation):
    n = seq // dilation
    shape = (batch, n, dilation * D_GROUP)
    spec = pl.BlockSpec((None, n, LANES), lambda b, c: (b, 0, c))
    o, lse = pl.pallas_call(
        functools.partial(_dilated_kernel, span=window // dilation),
        grid=(batch, dilation * D_GROUP // LANES),
        in_specs=[spec, spec, spec],
        out_specs=[spec, spec],
        out_shape=[jax.ShapeDtypeStruct(shape, F32), jax.ShapeDtypeStruct(shape, F32)],
        compiler_params=_params("parallel", "parallel"),
        name=f"dilated_d{dilation}",
    )(q.reshape(shape), k.reshape(shape), v.reshape(shape))
    return o.reshape(batch * seq, D_GROUP), lse.reshape(batch * seq, D_GROUP)


def _mix_out_kernel(x_ref, osb_ref, o1_ref, l1_ref, o2_ref, l2_ref, o3_ref, l3_ref,
                    gsb_ref, gdl_ref, w_ref, y_ref):
    l1, l2, l3 = l1_ref[...], l2_ref[...], l3_ref[...]
    m = jnp.maximum(jnp.maximum(l1, l2), l3)
    e1, e2, e3 = jnp.exp2(l1 - m), jnp.exp2(l2 - m), jnp.exp2(l3 - m)
    o_dl = (e1 * o1_ref[...] + e2 * o2_ref[...] + e3 * o3_ref[...]) / (e1 + e2 + e3)
    sb = (_rms(osb_ref[...]) * gsb_ref[...]).astype(BF16)
    dl = (_rms(o_dl) * gdl_ref[...]).astype(BF16)
    y = jnp.dot(sb, w_ref[:D_GROUP, :], preferred_element_type=F32)
    y = y + jnp.dot(dl, w_ref[D_GROUP:, :], preferred_element_type=F32)
    y_ref[...] = x_ref[...] + y


def _mix_out(x, o_sb, dl_parts, g_sb, g_dl, w_out):
    t = x.shape[0]
    row = lambda w: pl.BlockSpec((TM, w), lambda i: (i, 0))
    flat = [a for part in dl_parts for a in part]
    return pl.pallas_call(
        _mix_out_kernel,
        grid=(t // TM,),
        in_specs=[row(D_MODEL)] + [row(D_GROUP)] * 7
                 + [_resident((1, D_GROUP)), _resident((1, D_GROUP)), _resident((D_MODEL, D_MODEL))],
        out_specs=row(D_MODEL),
        out_shape=jax.ShapeDtypeStruct((t, D_MODEL), F32),
        compiler_params=_params("parallel"),
        name="mix_out",
    )(x, o_sb, *flat, g_sb.reshape(1, D_GROUP), g_dl.reshape(1, D_GROUP), w_out)


def _rope_tables(seq):
    half = HEAD_DIM // 2
    inv_freq = ROPE_THETA ** (-jnp.arange(half, dtype=F32) / half)
    ang = jnp.arange(seq).astype(F32)[:, None] * inv_freq[None, :]
    cos, sin = jnp.cos(ang), jnp.sin(ang)
    reps = LANES // HEAD_DIM
    cos_t = jnp.tile(jnp.concatenate([cos, cos], axis=1), (1, reps))
    sin_t = jnp.tile(jnp.concatenate([-sin, sin], axis=1), (1, reps))
    return cos_t, sin_t


def kernel(x, ffn1_norm, ffn1_w_gate, ffn1_w_up, ffn1_w_down, mix_norm, w_in, sb_out_norm, dil_out_norm, w_out, ffn2_norm, ffn2_w_gate, ffn2_w_up, ffn2_w_down, final_norm):
    batch, seq, _ = x.shape
    depth = ffn1_norm.shape[0]
    cos, sin = _rope_tables(seq)
    h = x.reshape(batch * seq, D_MODEL)
    for layer in range(depth):
        h = _ffn_block(h, ffn1_norm[layer], ffn1_w_gate[layer].astype(BF16),
                       ffn1_w_up[layer].astype(BF16), ffn1_w_down[layer].astype(BF16))
        q_sb, kt_sb, v_sb, q_dl, k_dl, v_dl = _in_proj(
            h, mix_norm[layer], w_in[layer].astype(BF16), cos, sin, batch, seq)
        o_sb = _sb_attention(q_sb, kt_sb, v_sb, batch, seq)
        dl_parts = [_dilated_pattern(q_dl, k_dl, v_dl, batch, seq, window, dilation)
                    for window, dilation in DILATED_PATTERNS]
        h = _mix_out(h, o_sb, dl_parts, sb_out_norm[layer], dil_out_norm[layer], w_out[layer].astype(BF16))
        last = layer == depth - 1
        h = _ffn_block(h, ffn2_norm[layer], ffn2_w_gate[layer].astype(BF16),
                       ffn2_w_up[layer].astype(BF16), ffn2_w_down[layer].astype(BF16),
                       final_gain=final_norm if last else None)
    return h.reshape(batch, seq, D_MODEL)
```

```python
import functools
import math

import jax
import jax.numpy as jnp
from jax import lax
from jax.experimental import pallas as pl
from jax.experimental.pallas import tpu as pltpu

D_MODEL = 1024
HEAD_DIM = 64
D_GROUP = 512
D_IN = 6 * D_GROUP
D_FF = 2816
DILATED_PATTERNS = ((128, 1), (512, 4), (2048, 16))
ROPE_THETA = 10000.0
RMS_EPS = 1e-6
HALF_STEP = 0.5

LANES = 128
LOG2E = math.log2(math.e)
INV_LN2 = 1.0 / math.log(2.0)
Q_SCALE = HEAD_DIM ** -0.5 * LOG2E

TM = 512
FF_CHUNK = 256
SB_TQ = 512
SB_KB = 256
DL_TQ = 128
VMEM_LIMIT = 48 * 1024 * 1024

F32 = jnp.float32
BF16 = jnp.bfloat16


def _rms(x):
    return x * lax.rsqrt(jnp.mean(x * x, axis=-1, keepdims=True) + RMS_EPS)


def _params(*sem):
    return pltpu.CompilerParams(dimension_semantics=sem, vmem_limit_bytes=VMEM_LIMIT)


def _resident(shape):
    return pl.BlockSpec(shape, lambda *_: (0,) * len(shape), pipeline_mode=pl.Buffered(1))


def _ffn_kernel(x_ref, gain_ref, wg_ref, wu_ref, wd_ref, *rest, final_norm):
    o_ref = rest[-1]
    x = x_ref[...]
    h = (_rms(x) * gain_ref[...]).astype(BF16)
    acc = jnp.zeros(x.shape, F32)
    for c in range(0, D_FF, FF_CHUNK):
        g = jnp.dot(h, wg_ref[:, c:c + FF_CHUNK], preferred_element_type=F32)
        u = jnp.dot(h, wu_ref[:, c:c + FF_CHUNK], preferred_element_type=F32)
        a = (g * jax.nn.sigmoid(g) * u).astype(BF16)
        acc = acc + jnp.dot(a, wd_ref[c:c + FF_CHUNK, :], preferred_element_type=F32)
    y = x + HALF_STEP * acc
    if final_norm:
        y = _rms(y) * rest[0][...]
    o_ref[...] = y


def _ffn_block(x, gain, wg, wu, wd, final_gain=None):
    t = x.shape[0]
    row = pl.BlockSpec((TM, D_MODEL), lambda i: (i, 0))
    vec = _resident((1, D_MODEL))
    in_specs = [row, vec, _resident((D_MODEL, D_FF)), _resident((D_MODEL, D_FF)), _resident((D_FF, D_MODEL))]
    args = [x, gain.reshape(1, D_MODEL), wg, wu, wd]
    if final_gain is not None:
        in_specs.append(vec)
        args.append(final_gain.reshape(1, D_MODEL))
    return pl.pallas_call(
        functools.partial(_ffn_kernel, final_norm=final_gain is not None),
        grid=(t // TM,),
        in_specs=in_specs,
        out_specs=row,
        out_shape=jax.ShapeDtypeStruct((t, D_MODEL), F32),
        compiler_params=_params("parallel"),
        name="ffn_block",
    )(*args)


def _in_proj_kernel(x_ref, gain_ref, w_ref, cos_ref, sin_ref,
                    qsb_ref, ktsb_ref, vsb_ref, qdl_ref, kdl_ref, vdl_ref):
    h = (_rms(x_ref[...]) * gain_ref[...]).astype(BF16)

    def proj(j):
        return jnp.dot(h, w_ref[:, j * D_GROUP:(j + 1) * D_GROUP], preferred_element_type=F32)

    cos = cos_ref[...]
    sin = sin_ref[...]
    lane = lax.broadcasted_iota(jnp.int32, cos.shape, 1)
    low_half = (lane % HEAD_DIM) < HEAD_DIM // 2

    def rope(t):
        outs = []
        for j in range(D_GROUP // LANES):
            tj = t[:, j * LANES:(j + 1) * LANES]
            ahead = pltpu.roll(tj, LANES - HEAD_DIM // 2, 1)
            behind = pltpu.roll(tj, HEAD_DIM // 2, 1)
            outs.append(tj * cos + jnp.where(low_half, ahead, behind) * sin)
        return jnp.concatenate(outs, axis=1)

    qsb_ref[...] = (proj(0) * Q_SCALE).astype(BF16)
    kt = proj(1).T
    for hp in range(D_GROUP // LANES):
        for kb in range(TM // SB_KB):
            ktsb_ref[hp, kb] = kt[hp * LANES:(hp + 1) * LANES, kb * SB_KB:(kb + 1) * SB_KB].astype(BF16)
    vsb_ref[...] = proj(2).astype(BF16)
    qdl_ref[...] = (rope(proj(3)) * Q_SCALE).astype(BF16)
    kdl_ref[...] = rope(proj(4)).astype(BF16)
    vdl_ref[...] = proj(5).astype(BF16)


def _in_proj(x, gain, w_in, cos, sin, batch, seq):
    t = x.shape[0]
    n_seq_tiles = seq // TM
    row = lambda w: pl.BlockSpec((TM, w), lambda b, i: (b * n_seq_tiles + i, 0))
    grp = jax.ShapeDtypeStruct((t, D_GROUP), BF16)
    kt_shape = (batch, D_GROUP // LANES, seq // SB_KB, LANES, SB_KB)
    return pl.pallas_call(
        _in_proj_kernel,
        grid=(batch, n_seq_tiles),
        in_specs=[row(D_MODEL), _resident((1, D_MODEL)), _resident((D_MODEL, D_IN)),
                  pl.BlockSpec((TM, LANES), lambda b, i: (i, 0)),
                  pl.BlockSpec((TM, LANES), lambda b, i: (i, 0))],
        out_specs=[row(D_GROUP),
                   pl.BlockSpec((None, D_GROUP // LANES, TM // SB_KB, LANES, SB_KB),
                                lambda b, i: (b, 0, i, 0, 0)),
                   row(D_GROUP), row(D_GROUP), row(D_GROUP), row(D_GROUP)],
        out_shape=[grp, jax.ShapeDtypeStruct(kt_shape, BF16), grp, grp, grp, grp],
        compiler_params=_params("parallel", "parallel"),
        name="in_proj",
    )(x, gain.reshape(1, D_MODEL), w_in, cos, sin)


def _sb_kernel(q_ref, kt_ref, v_ref, csum_ref, o_ref, qs_ref, acc_ref, run_ref):
    i = pl.program_id(2)
    n_heads = LANES // HEAD_DIM
    q = q_ref[...]
    lane = lax.broadcasted_iota(jnp.int32, (SB_TQ, LANES), 1)
    for hh in range(n_heads):
        in_head = (lane >= hh * HEAD_DIM) & (lane < (hh + 1) * HEAD_DIM)
        qs_ref[hh * SB_TQ:(hh + 1) * SB_TQ, :] = jnp.where(in_head, q, jnp.zeros_like(q))
    run_ref[...] = jnp.zeros_like(run_ref)
    acc_ref[...] = jnp.zeros_like(acc_ref)
    csum_w = csum_ref[...]
    sign_bit = jnp.uint32(0x80000000)

    def neg_cumsum(p_half):
        hi = p_half.astype(BF16)
        lo = (p_half - hi.astype(F32)).astype(BF16)
        return jnp.dot(jnp.concatenate([hi, lo], axis=1), csum_w, preferred_element_type=F32)

    def block(kb, diagonal):
        v = v_ref[pl.ds(pl.multiple_of(kb * SB_KB, SB_KB), SB_KB), :]
        w = jnp.dot(qs_ref[...], kt_ref[kb], preferred_element_type=F32)
        minus_abs = lax.bitcast_convert_type(lax.bitcast_convert_type(w, jnp.uint32) | sign_bit, F32)
        p = jnp.maximum(w, 0.0) + jnp.log(1.0 + jnp.exp2(minus_abs)) * INV_LN2
        if diagonal:
            row = lax.broadcasted_iota(jnp.int32, w.shape, 0) % SB_TQ
            col = lax.broadcasted_iota(jnp.int32, w.shape, 1)
            strictly_before = col + (kb * SB_KB - i * SB_TQ) < row
            p = jnp.where(strictly_before, p, 0.0)
        run = run_ref[...]
        c_new = neg_cumsum(p[:, LANES:])
        c_old = neg_cumsum(p[:, :LANES])
        run_mid = run + c_new[:, LANES:]
        x_new = w[:, LANES:] + c_new[:, :LANES] + run
        x_old = w[:, :LANES] + c_old[:, :LANES] + run_mid
        run_ref[...] = run_mid + c_old[:, LANES:]
        a = jnp.exp2(jnp.concatenate([x_old, x_new], axis=1))
        if diagonal:
            a = jnp.where(strictly_before, a, 0.0)
        acc_ref[...] += jnp.dot(a.astype(BF16), v, preferred_element_type=F32)

    blocks_per_tile = SB_TQ // SB_KB
    for d in range(blocks_per_tile):
        block(i * blocks_per_tile + (blocks_per_tile - 1 - d), True)

    def body(j, carry):
        block(i * blocks_per_tile - 1 - j, False)
        return carry

    lax.fori_loop(0, i * blocks_per_tile, body, 0)
    o_ref[...] = jnp.where(lane < HEAD_DIM, acc_ref[:SB_TQ, :], acc_ref[SB_TQ:, :])


def _csum_weights():
    j = jnp.arange(LANES)[:, None]
    s = jnp.arange(LANES)[None, :]
    tri = jnp.where(j >= s, -1.0, 0.0)
    half = jnp.concatenate([tri, jnp.full((LANES, LANES), -1.0)], axis=1)
    return jnp.concatenate([half, half], axis=0).astype(BF16)


def _sb_attention(q, kt, v, batch, seq):
    t = q.shape[0]
    n_q = seq // SB_TQ
    n_hp = D_GROUP // LANES
    return pl.pallas_call(
        _sb_kernel,
        grid=(batch, n_hp, n_q),
        in_specs=[pl.BlockSpec((SB_TQ, LANES), lambda b, hp, i: (b * n_q + i, hp)),
                  pl.BlockSpec((None, None, seq // SB_KB, LANES, SB_KB), lambda b, hp, i: (b, hp, 0, 0, 0)),
                  pl.BlockSpec((seq, LANES), lambda b, hp, i: (b, hp)),
                  _resident((2 * LANES, 2 * LANES))],
        out_specs=pl.BlockSpec((SB_TQ, LANES), lambda b, hp, i: (b * n_q + i, hp)),
        out_shape=jax.ShapeDtypeStruct((t, D_GROUP), F32),
        scratch_shapes=[pltpu.VMEM((2 * SB_TQ, LANES), BF16), pltpu.VMEM((2 * SB_TQ, LANES), F32),
                        pltpu.VMEM((2 * SB_TQ, LANES), F32)],
        compiler_params=_params("parallel", "parallel", "arbitrary"),
        name="sb_attention",
    )(q, kt, v, _csum_weights())


def _dilated_kernel(q_ref, k_ref, v_ref, o_ref, lse_ref, *, span):
    n_tiles = q_ref.shape[0] // DL_TQ
    lane = lax.broadcasted_iota(jnp.int32, (DL_TQ, LANES), 1)
    row = lax.broadcasted_iota(jnp.int32, (DL_TQ, 2 * DL_TQ), 0)
    col = lax.broadcasted_iota(jnp.int32, (DL_TQ, 2 * DL_TQ), 1)

    def tile(i, carry):
        q0 = pl.multiple_of(i * DL_TQ, DL_TQ)
        k0 = pl.multiple_of(jnp.maximum(i - 1, 0) * DL_TQ, DL_TQ)
        q = q_ref[pl.ds(q0, DL_TQ), :]
        k = k_ref[pl.ds(k0, 2 * DL_TQ), :]
        v = v_ref[pl.ds(k0, 2 * DL_TQ), :]
        dist = (q0 - k0) + row - col
        valid = (dist >= 0) & (dist <= span)
        outs, lses = [], []
        for hh in range(LANES // HEAD_DIM):
            in_head = (lane >= hh * HEAD_DIM) & (lane < (hh + 1) * HEAD_DIM)
            qh = jnp.where(in_head, q, jnp.zeros_like(q))
            z = lax.dot_general(qh, k, (((1,), (1,)), ((), ())), preferred_element_type=F32)
            z = jnp.where(valid, z, -jnp.inf)
            m = jnp.max(z, axis=-1, keepdims=True)
            p = jnp.exp2(z - m)
            denom = jnp.sum(p, axis=-1, keepdims=True)
            o = jnp.dot(p.astype(BF16), v, preferred_element_type=F32) / denom
            outs.append(o)
            lses.append(jnp.broadcast_to(m + jnp.log(denom) * INV_LN2, (DL_TQ, LANES)))
        o_ref[pl.ds(q0, DL_TQ), :] = jnp.where(lane < HEAD_DIM, outs[0], outs[1])
        lse_ref[pl.ds(q0, DL_TQ), :] = jnp.where(lane < HEAD_DIM, lses[0], lses[1])
        return carry

    lax.fori_loop(0, n_tiles, tile, 0)


def _dilated_pattern(q, k, v, batch, seq, window, dilation):
    n = seq // dilation
    shape = (batch, n, dilation * D_GROUP)
    spec = pl.BlockSpec((None, n, LANES), lambda b, c: (b, 0, c))
    o, lse = pl.pallas_call(
        functools.partial(_dilated_kernel, span=window // dilation),
        grid=(batch, dilation * D_GROUP // LANES),
        in_specs=[spec, spec, spec],
        out_specs=[spec, spec],
        out_shape=[jax.ShapeDtypeStruct(shape, F32), jax.ShapeDtypeStruct(shape, F32)],
        compiler_params=_params("parallel", "parallel"),
        name=f"dilated_d{dilation}",
    )(q.reshape(shape), k.reshape(shape), v.reshape(shape))
    return o.reshape(batch * seq, D_GROUP), lse.reshape(batch * seq, D_GROUP)


def _mix_out_kernel(x_ref, osb_ref, o1_ref, l1_ref, o2_ref, l2_ref, o3_ref, l3_ref,
                    gsb_ref, gdl_ref, w_ref, y_ref):
    l1, l2, l3 = l1_ref[...], l2_ref[...], l3_ref[...]
    m = jnp.maximum(jnp.maximum(l1, l2), l3)
    e1, e2, e3 = jnp.exp2(l1 - m), jnp.exp2(l2 - m), jnp.exp2(l3 - m)
    o_dl = (e1 * o1_ref[...] + e2 * o2_ref[...] + e3 * o3_ref[...]) / (e1 + e2 + e3)
    sb = (_rms(osb_ref[...]) * gsb_ref[...]).astype(BF16)
    dl = (_rms(o_dl) * gdl_ref[...]).astype(BF16)
    y = jnp.dot(sb, w_ref[:D_GROUP, :], preferred_element_type=F32)
    y = y + jnp.dot(dl, w_ref[D_GROUP:, :], preferred_element_type=F32)
    y_ref[...] = x_ref[...] + y


def _mix_out(x, o_sb, dl_parts, g_sb, g_dl, w_out):
    t = x.shape[0]
    row = lambda w: pl.BlockSpec((TM, w), lambda i: (i, 0))
    flat = [a for part in dl_parts for a in part]
    return pl.pallas_call(
        _mix_out_kernel,
        grid=(t // TM,),
        in_specs=[row(D_MODEL)] + [row(D_GROUP)] * 7
                 + [_resident((1, D_GROUP)), _resident((1, D_GROUP)), _resident((D_MODEL, D_MODEL))],
        out_specs=row(D_MODEL),
        out_shape=jax.ShapeDtypeStruct((t, D_MODEL), F32),
        compiler_params=_params("parallel"),
        name="mix_out",
    )(x, o_sb, *flat, g_sb.reshape(1, D_GROUP), g_dl.reshape(1, D_GROUP), w_out)


def _rope_tables(seq):
    half = HEAD_DIM // 2
    inv_freq = ROPE_THETA ** (-jnp.arange(half, dtype=F32) / half)
    ang = jnp.arange(seq).astype(F32)[:, None] * inv_freq[None, :]
    cos, sin = jnp.cos(ang), jnp.sin(ang)
    reps = LANES // HEAD_DIM
    cos_t = jnp.tile(jnp.concatenate([cos, cos], axis=1), (1, reps))
    sin_t = jnp.tile(jnp.concatenate([-sin, sin], axis=1), (1, reps))
    return cos_t, sin_t


def kernel(x, ffn1_norm, ffn1_w_gate, ffn1_w_up, ffn1_w_down, mix_norm, w_in, sb_out_norm, dil_out_norm, w_out, ffn2_norm, ffn2_w_gate, ffn2_w_up, ffn2_w_down, final_norm):
    batch, seq, _ = x.shape
    depth = ffn1_norm.shape[0]
    cos, sin = _rope_tables(seq)
    h = x.reshape(batch * seq, D_MODEL)
    for layer in range(depth):
        h = _ffn_block(h, ffn1_norm[layer], ffn1_w_gate[layer].astype(BF16),
                       ffn1_w_up[layer].astype(BF16), ffn1_w_down[layer].astype(BF16))
        q_sb, kt_sb, v_sb, q_dl, k_dl, v_dl = _in_proj(
            h, mix_norm[layer], w_in[layer].astype(BF16), cos, sin, batch, seq)
        o_sb = _sb_attention(q_sb, kt_sb, v_sb, batch, seq)
        dl_parts = [_dilated_pattern(q_dl, k_dl, v_dl, batch, seq, window, dilation)
                    for window, dilation in DILATED_PATTERNS]
        h = _mix_out(h, o_sb, dl_parts, sb_out_norm[layer], dil_out_norm[layer], w_out[layer].astype(BF16))
        last = layer == depth - 1
        h = _ffn_block(h, ffn2_norm[layer], ffn2_w_gate[layer].astype(BF16),
                       ffn2_w_up[layer].astype(BF16), ffn2_w_down[layer].astype(BF16),
                       final_gain=final_norm if last else None)
    return h.reshape(batch, seq, D_MODEL)
```

```python
import functools
import math

import jax
import jax.numpy as jnp
from jax import lax
from jax.experimental import pallas as pl
from jax.experimental.pallas import tpu as pltpu

D_MODEL = 1024
HEAD_DIM = 64
D_GROUP = 512
D_IN = 6 * D_GROUP
D_FF = 2816
DILATED_PATTERNS = ((128, 1), (512, 4), (2048, 16))
ROPE_THETA = 10000.0
RMS_EPS = 1e-6
HALF_STEP = 0.5

LANES = 128
LOG2E = math.log2(math.e)
INV_LN2 = 1.0 / math.log(2.0)
Q_SCALE = HEAD_DIM ** -0.5 * LOG2E

TM = 512
FF_CHUNK = 256
SB_TQ = 512
SB_KB = 256
SB_BLOCKS_PER_TILE = SB_TQ // SB_KB
assert SB_BLOCKS_PER_TILE == 2
DL_TQ = 128
DL_GROUP = 3
VMEM_LIMIT = 48 * 1024 * 1024

F32 = jnp.float32
BF16 = jnp.bfloat16


def _rms(x):
    return x * lax.rsqrt(jnp.mean(x * x, axis=-1, keepdims=True) + RMS_EPS)


def _params(*sem):
    return pltpu.CompilerParams(dimension_semantics=sem, vmem_limit_bytes=VMEM_LIMIT)


def _resident(shape):
    return pl.BlockSpec(shape, lambda *_: (0,) * len(shape), pipeline_mode=pl.Buffered(1))


def _ffn_kernel(x_ref, gain_ref, wg_ref, wu_ref, wd_ref, *rest, final_norm):
    o_ref = rest[-1]
    x = x_ref[...]
    h = (_rms(x) * gain_ref[...]).astype(BF16)
    acc = jnp.zeros(x.shape, F32)
    for c in range(0, D_FF, FF_CHUNK):
        g = jnp.dot(h, wg_ref[:, c:c + FF_CHUNK], preferred_element_type=F32)
        u = jnp.dot(h, wu_ref[:, c:c + FF_CHUNK], preferred_element_type=F32)
        a = (g * jax.nn.sigmoid(g) * u).astype(BF16)
        acc = acc + jnp.dot(a, wd_ref[c:c + FF_CHUNK, :], preferred_element_type=F32)
    y = x + HALF_STEP * acc
    if final_norm:
        y = _rms(y) * rest[0][...]
    o_ref[...] = y


def _ffn_block(x, gain, wg, wu, wd, final_gain=None):
    t = x.shape[0]
    row = pl.BlockSpec((TM, D_MODEL), lambda i: (i, 0))
    vec = _resident((1, D_MODEL))
    in_specs = [row, vec, _resident((D_MODEL, D_FF)), _resident((D_MODEL, D_FF)), _resident((D_FF, D_MODEL))]
    args = [x, gain.reshape(1, D_MODEL), wg, wu, wd]
    if final_gain is not None:
        in_specs.append(vec)
        args.append(final_gain.reshape(1, D_MODEL))
    return pl.pallas_call(
        functools.partial(_ffn_kernel, final_norm=final_gain is not None),
        grid=(t // TM,),
        in_specs=in_specs,
        out_specs=row,
        out_shape=jax.ShapeDtypeStruct((t, D_MODEL), F32),
        compiler_params=_params("parallel"),
        name="ffn_block",
    )(*args)


def _in_proj_kernel(x_ref, gain_ref, w_ref, cos_ref, sin_ref,
                    qsb_ref, ktsb_ref, vsb_ref, qdl_ref, kdl_ref, vdl_ref):
    h = (_rms(x_ref[...]) * gain_ref[...]).astype(BF16)

    def proj(j):
        return jnp.dot(h, w_ref[:, j * D_GROUP:(j + 1) * D_GROUP], preferred_element_type=F32)

    cos = cos_ref[...]
    sin = sin_ref[...]
    lane = lax.broadcasted_iota(jnp.int32, cos.shape, 1)
    low_half = (lane % HEAD_DIM) < HEAD_DIM // 2

    def rope(t):
        outs = []
        for j in range(D_GROUP // LANES):
            tj = t[:, j * LANES:(j + 1) * LANES]
            ahead = pltpu.roll(tj, LANES - HEAD_DIM // 2, 1)
            behind = pltpu.roll(tj, HEAD_DIM // 2, 1)
            outs.append(tj * cos + jnp.where(low_half, ahead, behind) * sin)
        return jnp.concatenate(outs, axis=1)

    qsb_ref[...] = (proj(0) * Q_SCALE).astype(BF16)
    kt = proj(1).T
    for hp in range(D_GROUP // LANES):
        for kb in range(TM // SB_KB):
            ktsb_ref[hp, kb] = kt[hp * LANES:(hp + 1) * LANES, kb * SB_KB:(kb + 1) * SB_KB].astype(BF16)
    vsb_ref[...] = proj(2).astype(BF16)
    qdl_ref[...] = (rope(proj(3)) * Q_SCALE).astype(BF16)
    kdl_ref[...] = rope(proj(4)).astype(BF16)
    vdl_ref[...] = proj(5).astype(BF16)


def _in_proj(x, gain, w_in, cos, sin, batch, seq):
    t = x.shape[0]
    n_seq_tiles = seq // TM
    row = lambda w: pl.BlockSpec((TM, w), lambda b, i: (b * n_seq_tiles + i, 0))
    grp = jax.ShapeDtypeStruct((t, D_GROUP), BF16)
    kt_shape = (batch, D_GROUP // LANES, seq // SB_KB, LANES, SB_KB)
    return pl.pallas_call(
        _in_proj_kernel,
        grid=(batch, n_seq_tiles),
        in_specs=[row(D_MODEL), _resident((1, D_MODEL)), _resident((D_MODEL, D_IN)),
                  pl.BlockSpec((TM, LANES), lambda b, i: (i, 0)),
                  pl.BlockSpec((TM, LANES), lambda b, i: (i, 0))],
        out_specs=[row(D_GROUP),
                   pl.BlockSpec((None, D_GROUP // LANES, TM // SB_KB, LANES, SB_KB),
                                lambda b, i: (b, 0, i, 0, 0)),
                   row(D_GROUP), row(D_GROUP), row(D_GROUP), row(D_GROUP)],
        out_shape=[grp, jax.ShapeDtypeStruct(kt_shape, BF16), grp, grp, grp, grp],
        compiler_params=_params("parallel", "parallel"),
        name="in_proj",
    )(x, gain.reshape(1, D_MODEL), w_in, cos, sin)


def _sb_kernel(q_ref, kt_ref, v_ref, csum_ref, o_ref, qs_ref, acc_ref, run_ref, w_ref, a_ref):
    i = pl.program_id(2)
    n_heads = LANES // HEAD_DIM
    q = q_ref[...]
    lane = lax.broadcasted_iota(jnp.int32, (SB_TQ, LANES), 1)
    for hh in range(n_heads):
        in_head = (lane >= hh * HEAD_DIM) & (lane < (hh + 1) * HEAD_DIM)
        qs_ref[hh * SB_TQ:(hh + 1) * SB_TQ, :] = jnp.where(in_head, q, jnp.zeros_like(q))
    run_ref[...] = jnp.zeros_like(run_ref)
    acc_ref[...] = jnp.zeros_like(acc_ref)
    a_ref[1] = jnp.zeros(a_ref.shape[1:], a_ref.dtype)
    csum_w = csum_ref[...]

    def scores(kb):
        return jnp.dot(qs_ref[...], kt_ref[kb], preferred_element_type=F32)

    def weights(w, mask):
        p = jnp.maximum(w, 0.0) + jnp.log(1.0 + jnp.exp2(-jnp.abs(w))) * INV_LN2
        if mask is not None:
            p = jnp.where(mask, p, 0.0)
        neg_c = jnp.dot(p.astype(BF16), csum_w, preferred_element_type=F32)
        run = run_ref[...]
        a = jnp.exp2(w + neg_c + jnp.concatenate([run] * (SB_KB // LANES), axis=1))
        if mask is not None:
            a = jnp.where(mask, a, 0.0)
        run_ref[...] = run + jnp.broadcast_to(neg_c[:, 0:1], run.shape)
        return a.astype(BF16)

    def accumulate(a, kb):
        v = v_ref[pl.ds(pl.multiple_of(kb * SB_KB, SB_KB), SB_KB), :]
        acc_ref[...] += jnp.dot(a, v, preferred_element_type=F32)

    first_old = i * SB_BLOCKS_PER_TILE - 1
    w_ref[0] = scores(jnp.maximum(first_old, 0))

    for d in range(SB_BLOCKS_PER_TILE):
        kb = i * SB_BLOCKS_PER_TILE + (SB_BLOCKS_PER_TILE - 1 - d)
        row = lax.broadcasted_iota(jnp.int32, (n_heads * SB_TQ, SB_KB), 0) % SB_TQ
        col = lax.broadcasted_iota(jnp.int32, (n_heads * SB_TQ, SB_KB), 1)
        strictly_before = col + (kb * SB_KB - i * SB_TQ) < row
        accumulate(weights(scores(kb), strictly_before), kb)

    def two_steps(u, carry):
        for s in range(2):
            kb = first_old - (2 * u + s)
            w_ref[1 - s] = scores(jnp.maximum(kb - 1, 0))
            accumulate(a_ref[1 - s], kb + 1)
            a_ref[s] = weights(w_ref[s], None)
        return carry

    lax.fori_loop(0, i, two_steps, 0)
    accumulate(a_ref[1], 0)
    o_ref[...] = jnp.where(lane < HEAD_DIM, acc_ref[:SB_TQ, :], acc_ref[SB_TQ:, :])


def _csum_weights():
    j = jnp.arange(SB_KB)[:, None]
    s = jnp.arange(SB_KB)[None, :]
    return jnp.where(j >= s, -1.0, 0.0).astype(BF16)


def _sb_attention(q, kt, v, batch, seq):
    t = q.shape[0]
    n_q = seq // SB_TQ
    n_hp = D_GROUP // LANES
    return pl.pallas_call(
        _sb_kernel,
        grid=(batch, n_hp, n_q),
        in_specs=[pl.BlockSpec((SB_TQ, LANES), lambda b, hp, i: (b * n_q + i, hp)),
                  pl.BlockSpec((None, None, seq // SB_KB, LANES, SB_KB), lambda b, hp, i: (b, hp, 0, 0, 0)),
                  pl.BlockSpec((seq, LANES), lambda b, hp, i: (b, hp)),
                  _resident((SB_KB, SB_KB))],
        out_specs=pl.BlockSpec((SB_TQ, LANES), lambda b, hp, i: (b * n_q + i, hp)),
        out_shape=jax.ShapeDtypeStruct((t, D_GROUP), F32),
        scratch_shapes=[pltpu.VMEM((2 * SB_TQ, LANES), BF16),
                        pltpu.VMEM((2 * SB_TQ, LANES), F32),
                        pltpu.VMEM((2 * SB_TQ, LANES), F32),
                        pltpu.VMEM((2, 2 * SB_TQ, SB_KB), F32),
                        pltpu.VMEM((2, 2 * SB_TQ, SB_KB), BF16)],
        compiler_params=_params("parallel", "parallel", "arbitrary"),
        name="sb_attention",
    )(q, kt, v, _csum_weights())


def _dilated_kernel(q_ref, k_ref, v_ref, o_ref, lse_ref, *, span):
    n_tiles = q_ref.shape[0] // DL_TQ
    lane = lax.broadcasted_iota(jnp.int32, (DL_TQ, LANES), 1)
    row = lax.broadcasted_iota(jnp.int32, (DL_TQ, 2 * DL_TQ), 0)
    col = lax.broadcasted_iota(jnp.int32, (DL_TQ, 2 * DL_TQ), 1)

    def band(offset):
        dist = offset + row - col
        return (dist >= 0) & (dist <= span)

    band_first = band(0)
    band_rest = band(DL_TQ)

    def tile(q0, k0, valid):
        q = q_ref[pl.ds(q0, DL_TQ), :]
        k = k_ref[pl.ds(k0, 2 * DL_TQ), :]
        v = v_ref[pl.ds(k0, 2 * DL_TQ), :]
        outs, lses = [], []
        for hh in range(LANES // HEAD_DIM):
            in_head = (lane >= hh * HEAD_DIM) & (lane < (hh + 1) * HEAD_DIM)
            qh = jnp.where(in_head, q, jnp.zeros_like(q))
            z = lax.dot_general(qh, k, (((1,), (1,)), ((), ())), preferred_element_type=F32)
            z = jnp.where(valid, z, -jnp.inf)
            m = jnp.max(z, axis=-1, keepdims=True)
            p = jnp.exp2(z - m)
            denom = jnp.sum(p, axis=-1, keepdims=True)
            o = jnp.dot(p.astype(BF16), v, preferred_element_type=F32) / denom
            outs.append(o)
            lses.append(jnp.broadcast_to(m + jnp.log(denom) * INV_LN2, (DL_TQ, LANES)))
        o_ref[pl.ds(q0, DL_TQ), :] = jnp.where(lane < HEAD_DIM, outs[0], outs[1])
        lse_ref[pl.ds(q0, DL_TQ), :] = jnp.where(lane < HEAD_DIM, lses[0], lses[1])

    tile(0, 0, band_first)

    def group(g, carry):
        for u in range(DL_GROUP):
            k0 = pl.multiple_of((g * DL_GROUP + u) * DL_TQ, DL_TQ)
            tile(k0 + DL_TQ, k0, band_rest)
        return carry

    lax.fori_loop(0, (n_tiles - 1) // DL_GROUP, group, 0)


def _dilated_pattern(q, k, v, batch, seq, window, dilation):
    n = seq // dilation
    assert seq % (dilation * DL_TQ) == 0 and (n // DL_TQ - 1) % DL_GROUP == 0 and n >= 2 * DL_TQ
    shape = (batch, n, dilation * D_GROUP)
    spec = pl.BlockSpec((None, n, LANES), lambda b, c: (b, 0, c))
    o, lse = pl.pallas_call(
        functools.partial(_dilated_kernel, span=window // dilation),
        grid=(batch, dilation * D_GROUP // LANES),
        in_specs=[spec, spec, spec],
        out_specs=[spec, spec],
        out_shape=[jax.ShapeDtypeStruct(shape, F32), jax.ShapeDtypeStruct(shape, F32)],
        compiler_params=_params("parallel", "parallel"),
        name=f"dilated_d{dilation}",
    )(q.reshape(shape), k.reshape(shape), v.reshape(shape))
    return o.reshape(batch * seq, D_GROUP), lse.reshape(batch * seq, D_GROUP)


def _mix_out_kernel(x_ref, osb_ref, o1_ref, l1_ref, o2_ref, l2_ref, o3_ref, l3_ref,
                    gsb_ref, gdl_ref, w_ref, y_ref):
    l1, l2, l3 = l1_ref[...], l2_ref[...], l3_ref[...]
    m = jnp.maximum(jnp.maximum(l1, l2), l3)
    e1, e2, e3 = jnp.exp2(l1 - m), jnp.exp2(l2 - m), jnp.exp2(l3 - m)
    o_dl = (e1 * o1_ref[...] + e2 * o2_ref[...] + e3 * o3_ref[...]) / (e1 + e2 + e3)
    sb = (_rms(osb_ref[...]) * gsb_ref[...]).astype(BF16)
    dl = (_rms(o_dl) * gdl_ref[...]).astype(BF16)
    y = jnp.dot(sb, w_ref[:D_GROUP, :], preferred_element_type=F32)
    y = y + jnp.dot(dl, w_ref[D_GROUP:, :], preferred_element_type=F32)
    y_ref[...] = x_ref[...] + y


def _mix_out(x, o_sb, dl_parts, g_sb, g_dl, w_out):
    t = x.shape[0]
    row = lambda w: pl.BlockSpec((TM, w), lambda i: (i, 0))
    flat = [a for part in dl_parts for a in part]
    return pl.pallas_call(
        _mix_out_kernel,
        grid=(t // TM,),
        in_specs=[row(D_MODEL)] + [row(D_GROUP)] * 7
                 + [_resident((1, D_GROUP)), _resident((1, D_GROUP)), _resident((D_MODEL, D_MODEL))],
        out_specs=row(D_MODEL),
        out_shape=jax.ShapeDtypeStruct((t, D_MODEL), F32),
        compiler_params=_params("parallel"),
        name="mix_out",
    )(x, o_sb, *flat, g_sb.reshape(1, D_GROUP), g_dl.reshape(1, D_GROUP), w_out)


def _rope_tables(seq):
    half = HEAD_DIM // 2
    inv_freq = ROPE_THETA ** (-jnp.arange(half, dtype=F32) / half)
    ang = jnp.arange(seq).astype(F32)[:, None] * inv_freq[None, :]
    cos, sin = jnp.cos(ang), jnp.sin(ang)
    reps = LANES // HEAD_DIM
    cos_t = jnp.tile(jnp.concatenate([cos, cos], axis=1), (1, reps))
    sin_t = jnp.tile(jnp.concatenate([-sin, sin], axis=1), (1, reps))
    return cos_t, sin_t


def kernel(x, ffn1_norm, ffn1_w_gate, ffn1_w_up, ffn1_w_down, mix_norm, w_in, sb_out_norm, dil_out_norm, w_out, ffn2_norm, ffn2_w_gate, ffn2_w_up, ffn2_w_down, final_norm):
    batch, seq, _ = x.shape
    depth = ffn1_norm.shape[0]
    cos, sin = _rope_tables(seq)
    h = x.reshape(batch * seq, D_MODEL)
    for layer in range(depth):
        h = _ffn_block(h, ffn1_norm[layer], ffn1_w_gate[layer].astype(BF16),
                       ffn1_w_up[layer].astype(BF16), ffn1_w_down[layer].astype(BF16))
        q_sb, kt_sb, v_sb, q_dl, k_dl, v_dl = _in_proj(
            h, mix_norm[layer], w_in[layer].astype(BF16), cos, sin, batch, seq)
        o_sb = _sb_attention(q_sb, kt_sb, v_sb, batch, seq)
        dl_parts = [_dilated_pattern(q_dl, k_dl, v_dl, batch, seq, window, dilation)
                    for window, dilation in DILATED_PATTERNS]
        h = _mix_out(h, o_sb, dl_parts, sb_out_norm[layer], dil_out_norm[layer], w_out[layer].astype(BF16))
        last = layer == depth - 1
        h = _ffn_block(h, ffn2_norm[layer], ffn2_w_gate[layer].astype(BF16),
                       ffn2_w_up[layer].astype(BF16), ffn2_w_down[layer].astype(BF16),
                       final_gain=final_norm if last else None)
    return h.reshape(batch, seq, D_MODEL)
```

```python
import functools
import math

import jax
import jax.numpy as jnp
from jax import lax
from jax.experimental import pallas as pl
from jax.experimental.pallas import tpu as pltpu

D_MODEL = 1024
HEAD_DIM = 64
D_GROUP = 512
D_IN = 6 * D_GROUP
D_FF = 2816
DILATED_PATTERNS = ((128, 1), (512, 4), (2048, 16))
ROPE_THETA = 10000.0
RMS_EPS = 1e-6
HALF_STEP = 0.5

LANES = 128
LOG2E = math.log2(math.e)
INV_LN2 = 1.0 / math.log(2.0)
Q_SCALE = HEAD_DIM ** -0.5 * LOG2E

TM = 512
FF_CHUNK = 256
SB_TQ = 512
SB_KB = 256
SB_DEAD_LOG2 = -160.0
SB_BLOCKS_PER_TILE = SB_TQ // SB_KB
assert SB_BLOCKS_PER_TILE == 2
DL_TQ = 128
DL_GROUP = 3
VMEM_LIMIT = 48 * 1024 * 1024

F32 = jnp.float32
BF16 = jnp.bfloat16


def _rms(x):
    return x * lax.rsqrt(jnp.mean(x * x, axis=-1, keepdims=True) + RMS_EPS)


def _params(*sem):
    return pltpu.CompilerParams(dimension_semantics=sem, vmem_limit_bytes=VMEM_LIMIT)


def _resident(shape):
    return pl.BlockSpec(shape, lambda *_: (0,) * len(shape), pipeline_mode=pl.Buffered(1))


def _ffn_kernel(x_ref, gain_ref, wg_ref, wu_ref, wd_ref, *rest, final_norm):
    o_ref = rest[-1]
    x = x_ref[...]
    h = (_rms(x) * gain_ref[...]).astype(BF16)
    acc = jnp.zeros(x.shape, F32)
    for c in range(0, D_FF, FF_CHUNK):
        g = jnp.dot(h, wg_ref[:, c:c + FF_CHUNK], preferred_element_type=F32)
        u = jnp.dot(h, wu_ref[:, c:c + FF_CHUNK], preferred_element_type=F32)
        a = (g * jax.nn.sigmoid(g) * u).astype(BF16)
        acc = acc + jnp.dot(a, wd_ref[c:c + FF_CHUNK, :], preferred_element_type=F32)
    y = x + HALF_STEP * acc
    if final_norm:
        y = _rms(y) * rest[0][...]
    o_ref[...] = y


def _ffn_block(x, gain, wg, wu, wd, final_gain=None):
    t = x.shape[0]
    row = pl.BlockSpec((TM, D_MODEL), lambda i: (i, 0))
    vec = _resident((1, D_MODEL))
    in_specs = [row, vec, _resident((D_MODEL, D_FF)), _resident((D_MODEL, D_FF)), _resident((D_FF, D_MODEL))]
    args = [x, gain.reshape(1, D_MODEL), wg, wu, wd]
    if final_gain is not None:
        in_specs.append(vec)
        args.append(final_gain.reshape(1, D_MODEL))
    return pl.pallas_call(
        functools.partial(_ffn_kernel, final_norm=final_gain is not None),
        grid=(t // TM,),
        in_specs=in_specs,
        out_specs=row,
        out_shape=jax.ShapeDtypeStruct((t, D_MODEL), F32),
        compiler_params=_params("parallel"),
        name="ffn_block",
    )(*args)


def _in_proj_kernel(x_ref, gain_ref, w_ref, cos_ref, sin_ref,
                    qsb_ref, ktsb_ref, vsb_ref, qdl_ref, kdl_ref, vdl_ref):
    h = (_rms(x_ref[...]) * gain_ref[...]).astype(BF16)

    def proj(j):
        return jnp.dot(h, w_ref[:, j * D_GROUP:(j + 1) * D_GROUP], preferred_element_type=F32)

    cos = cos_ref[...]
    sin = sin_ref[...]
    lane = lax.broadcasted_iota(jnp.int32, cos.shape, 1)
    low_half = (lane % HEAD_DIM) < HEAD_DIM // 2

    def rope(t):
        outs = []
        for j in range(D_GROUP // LANES):
            tj = t[:, j * LANES:(j + 1) * LANES]
            ahead = pltpu.roll(tj, LANES - HEAD_DIM // 2, 1)
            behind = pltpu.roll(tj, HEAD_DIM // 2, 1)
            outs.append(tj * cos + jnp.where(low_half, ahead, behind) * sin)
        return jnp.concatenate(outs, axis=1)

    qsb_ref[...] = (proj(0) * Q_SCALE).astype(BF16)
    kt = proj(1).T
    for hp in range(D_GROUP // LANES):
        for kb in range(TM // SB_KB):
            ktsb_ref[hp, kb] = kt[hp * LANES:(hp + 1) * LANES, kb * SB_KB:(kb + 1) * SB_KB].astype(BF16)
    vsb_ref[...] = proj(2).astype(BF16)
    qdl_ref[...] = (rope(proj(3)) * Q_SCALE).astype(BF16)
    kdl_ref[...] = rope(proj(4)).astype(BF16)
    vdl_ref[...] = proj(5).astype(BF16)


def _in_proj(x, gain, w_in, cos, sin, batch, seq):
    t = x.shape[0]
    n_seq_tiles = seq // TM
    row = lambda w: pl.BlockSpec((TM, w), lambda b, i: (b * n_seq_tiles + i, 0))
    grp = jax.ShapeDtypeStruct((t, D_GROUP), BF16)
    kt_shape = (batch, D_GROUP // LANES, seq // SB_KB, LANES, SB_KB)
    return pl.pallas_call(
        _in_proj_kernel,
        grid=(batch, n_seq_tiles),
        in_specs=[row(D_MODEL), _resident((1, D_MODEL)), _resident((D_MODEL, D_IN)),
                  pl.BlockSpec((TM, LANES), lambda b, i: (i, 0)),
                  pl.BlockSpec((TM, LANES), lambda b, i: (i, 0))],
        out_specs=[row(D_GROUP),
                   pl.BlockSpec((None, D_GROUP // LANES, TM // SB_KB, LANES, SB_KB),
                                lambda b, i: (b, 0, i, 0, 0)),
                   row(D_GROUP), row(D_GROUP), row(D_GROUP), row(D_GROUP)],
        out_shape=[grp, jax.ShapeDtypeStruct(kt_shape, BF16), grp, grp, grp, grp],
        compiler_params=_params("parallel", "parallel"),
        name="in_proj",
    )(x, gain.reshape(1, D_MODEL), w_in, cos, sin)


def _sb_kernel(q_ref, kt_ref, v_ref, csum_ref, o_ref, qs_ref, acc_ref, run_ref, w_ref, a_ref):
    i = pl.program_id(2)
    n_heads = LANES // HEAD_DIM
    q = q_ref[...]
    lane = lax.broadcasted_iota(jnp.int32, (SB_TQ, LANES), 1)
    for hh in range(n_heads):
        in_head = (lane >= hh * HEAD_DIM) & (lane < (hh + 1) * HEAD_DIM)
        qs_ref[hh * SB_TQ:(hh + 1) * SB_TQ, :] = jnp.where(in_head, q, jnp.zeros_like(q))
    run_ref[...] = jnp.zeros_like(run_ref)
    acc_ref[...] = jnp.zeros_like(acc_ref)
    a_ref[1] = jnp.zeros(a_ref.shape[1:], a_ref.dtype)
    csum_w = csum_ref[...]

    def scores(kb):
        return jnp.dot(qs_ref[...], kt_ref[kb], preferred_element_type=F32)

    def weights(w, mask):
        p = jnp.maximum(w, 0.0) + jnp.log(1.0 + jnp.exp2(-jnp.abs(w))) * INV_LN2
        if mask is not None:
            p = jnp.where(mask, p, 0.0)
        neg_c = jnp.dot(p.astype(BF16), csum_w, preferred_element_type=F32)
        run = run_ref[...]
        a = jnp.exp2((w - p) + neg_c + jnp.concatenate([run] * (SB_KB // LANES), axis=1))
        if mask is not None:
            a = jnp.where(mask, a, 0.0)
        run_ref[...] = run + jnp.broadcast_to(neg_c[:, 0:1] - p[:, 0:1], run.shape)
        return a.astype(BF16)

    def accumulate(a, kb):
        v = v_ref[pl.ds(pl.multiple_of(kb * SB_KB, SB_KB), SB_KB), :]
        acc_ref[...] += jnp.dot(a, v, preferred_element_type=F32)

    first_old = i * SB_BLOCKS_PER_TILE - 1
    w_ref[0] = scores(jnp.maximum(first_old, 0))

    row = lax.broadcasted_iota(jnp.int32, (n_heads * SB_TQ, SB_KB), 0) % SB_TQ
    col = lax.broadcasted_iota(jnp.int32, (n_heads * SB_TQ, SB_KB), 1)
    for d in range(SB_BLOCKS_PER_TILE):
        kb_in_tile = SB_BLOCKS_PER_TILE - 1 - d
        strictly_before = col + kb_in_tile * SB_KB < row
        kb = i * SB_BLOCKS_PER_TILE + kb_in_tile
        accumulate(weights(scores(kb), strictly_before), kb)

    def step(kb, s):
        w_ref[1 - s] = scores(jnp.maximum(kb - 1, 0))
        accumulate(a_ref[1 - s], kb + 1)
        a_ref[s] = weights(w_ref[s], None)

    def two_steps(carry):
        u, _ = carry
        kb = first_old - 2 * u
        step(kb, 0)
        alive = (jnp.max(run_ref[...]) > SB_DEAD_LOG2).astype(jnp.int32)
        step(kb - 1, 1)
        return u + 1, alive

    n_pairs, _ = lax.while_loop(lambda c: (c[0] < i) & (c[1] > 0), two_steps, (jnp.int32(0), jnp.int32(1)))
    accumulate(a_ref[1], first_old + 1 - 2 * n_pairs)
    o_ref[...] = jnp.where(lane < HEAD_DIM, acc_ref[:SB_TQ, :], acc_ref[SB_TQ:, :])


def _csum_weights():
    j = jnp.arange(SB_KB)[:, None]
    s = jnp.arange(SB_KB)[None, :]
    return jnp.where(j > s, -1.0, 0.0).astype(BF16)


def _sb_attention(q, kt, v, batch, seq):
    t = q.shape[0]
    n_q = seq // SB_TQ
    n_hp = D_GROUP // LANES
    return pl.pallas_call(
        _sb_kernel,
        grid=(batch, n_hp, n_q),
        in_specs=[pl.BlockSpec((SB_TQ, LANES), lambda b, hp, i: (b * n_q + i, hp)),
                  pl.BlockSpec((None, None, seq // SB_KB, LANES, SB_KB), lambda b, hp, i: (b, hp, 0, 0, 0)),
                  pl.BlockSpec((seq, LANES), lambda b, hp, i: (b, hp)),
                  _resident((SB_KB, SB_KB))],
        out_specs=pl.BlockSpec((SB_TQ, LANES), lambda b, hp, i: (b * n_q + i, hp)),
        out_shape=jax.ShapeDtypeStruct((t, D_GROUP), F32),
        scratch_shapes=[pltpu.VMEM((2 * SB_TQ, LANES), BF16),
                        pltpu.VMEM((2 * SB_TQ, LANES), F32),
                        pltpu.VMEM((2 * SB_TQ, LANES), F32),
                        pltpu.VMEM((2, 2 * SB_TQ, SB_KB), F32),
                        pltpu.VMEM((2, 2 * SB_TQ, SB_KB), BF16)],
        compiler_params=_params("parallel", "parallel", "arbitrary"),
        name="sb_attention",
    )(q, kt, v, _csum_weights())


def _dilated_kernel(q_ref, k_ref, v_ref, o_ref, lse_ref, *, span):
    n_tiles = q_ref.shape[0] // DL_TQ
    lane = lax.broadcasted_iota(jnp.int32, (DL_TQ, LANES), 1)
    row = lax.broadcasted_iota(jnp.int32, (DL_TQ, 2 * DL_TQ), 0)
    col = lax.broadcasted_iota(jnp.int32, (DL_TQ, 2 * DL_TQ), 1)

    def band(offset):
        dist = offset + row - col
        return (dist >= 0) & (dist <= span)

    band_first = band(0)
    band_rest = band(DL_TQ)

    def tile(q0, k0, valid):
        q = q_ref[pl.ds(q0, DL_TQ), :]
        k = k_ref[pl.ds(k0, 2 * DL_TQ), :]
        v = v_ref[pl.ds(k0, 2 * DL_TQ), :]
        outs, lses = [], []
        for hh in range(LANES // HEAD_DIM):
            in_head = (lane >= hh * HEAD_DIM) & (lane < (hh + 1) * HEAD_DIM)
            qh = jnp.where(in_head, q, jnp.zeros_like(q))
            z = lax.dot_general(qh, k, (((1,), (1,)), ((), ())), preferred_element_type=F32)
            z = jnp.where(valid, z, -jnp.inf)
            m = jnp.max(z, axis=-1, keepdims=True)
            p = jnp.exp2(z - m)
            denom = jnp.sum(p, axis=-1, keepdims=True)
            o = jnp.dot(p.astype(BF16), v, preferred_element_type=F32) / denom
            outs.append(o)
            lses.append(jnp.broadcast_to(m + jnp.log(denom) * INV_LN2, (DL_TQ, LANES)))
        o_ref[pl.ds(q0, DL_TQ), :] = jnp.where(lane < HEAD_DIM, outs[0], outs[1])
        lse_ref[pl.ds(q0, DL_TQ), :] = jnp.where(lane < HEAD_DIM, lses[0], lses[1])

    tile(0, 0, band_first)

    def group(g, carry):
        for u in range(DL_GROUP):
            k0 = pl.multiple_of((g * DL_GROUP + u) * DL_TQ, DL_TQ)
            tile(k0 + DL_TQ, k0, band_rest)
        return carry

    lax.fori_loop(0, (n_tiles - 1) // DL_GROUP, group, 0)


def _dilated_pattern(q, k, v, batch, seq, window, dilation):
    n = seq // dilation
    assert seq % (dilation * DL_TQ) == 0 and (n // DL_TQ - 1) % DL_GROUP == 0 and n >= 2 * DL_TQ
    shape = (batch, n, dilation * D_GROUP)
    spec = pl.BlockSpec((None, n, LANES), lambda b, c: (b, 0, c))
    o, lse = pl.pallas_call(
        functools.partial(_dilated_kernel, span=window // dilation),
        grid=(batch, dilation * D_GROUP // LANES),
        in_specs=[spec, spec, spec],
        out_specs=[spec, spec],
        out_shape=[jax.ShapeDtypeStruct(shape, F32), jax.ShapeDtypeStruct(shape, F32)],
        compiler_params=_params("parallel", "parallel"),
        name=f"dilated_d{dilation}",
    )(q.reshape(shape), k.reshape(shape), v.reshape(shape))
    return o.reshape(batch * seq, D_GROUP), lse.reshape(batch * seq, D_GROUP)


def _mix_out_kernel(x_ref, osb_ref, o1_ref, l1_ref, o2_ref, l2_ref, o3_ref, l3_ref,
                    gsb_ref, gdl_ref, w_ref, y_ref):
    l1, l2, l3 = l1_ref[...], l2_ref[...], l3_ref[...]
    m = jnp.maximum(jnp.maximum(l1, l2), l3)
    e1, e2, e3 = jnp.exp2(l1 - m), jnp.exp2(l2 - m), jnp.exp2(l3 - m)
    o_dl = (e1 * o1_ref[...] + e2 * o2_ref[...] + e3 * o3_ref[...]) / (e1 + e2 + e3)
    sb = (_rms(osb_ref[...]) * gsb_ref[...]).astype(BF16)
    dl = (_rms(o_dl) * gdl_ref[...]).astype(BF16)
    y = jnp.dot(sb, w_ref[:D_GROUP, :], preferred_element_type=F32)
    y = y + jnp.dot(dl, w_ref[D_GROUP:, :], preferred_element_type=F32)
    y_ref[...] = x_ref[...] + y


def _mix_out(x, o_sb, dl_parts, g_sb, g_dl, w_out):
    t = x.shape[0]
    row = lambda w: pl.BlockSpec((TM, w), lambda i: (i, 0))
    flat = [a for part in dl_parts for a in part]
    return pl.pallas_call(
        _mix_out_kernel,
        grid=(t // TM,),
        in_specs=[row(D_MODEL)] + [row(D_GROUP)] * 7
                 + [_resident((1, D_GROUP)), _resident((1, D_GROUP)), _resident((D_MODEL, D_MODEL))],
        out_specs=row(D_MODEL),
        out_shape=jax.ShapeDtypeStruct((t, D_MODEL), F32),
        compiler_params=_params("parallel"),
        name="mix_out",
    )(x, o_sb, *flat, g_sb.reshape(1, D_GROUP), g_dl.reshape(1, D_GROUP), w_out)


def _rope_tables(seq):
    half = HEAD_DIM // 2
    inv_freq = ROPE_THETA ** (-jnp.arange(half, dtype=F32) / half)
    ang = jnp.arange(seq).astype(F32)[:, None] * inv_freq[None, :]
    cos, sin = jnp.cos(ang), jnp.sin(ang)
    reps = LANES // HEAD_DIM
    cos_t = jnp.tile(jnp.concatenate([cos, cos], axis=1), (1, reps))
    sin_t = jnp.tile(jnp.concatenate([-sin, sin], axis=1), (1, reps))
    return cos_t, sin_t


def kernel(x, ffn1_norm, ffn1_w_gate, ffn1_w_up, ffn1_w_down, mix_norm, w_in, sb_out_norm, dil_out_norm, w_out, ffn2_norm, ffn2_w_gate, ffn2_w_up, ffn2_w_down, final_norm):
    batch, seq, _ = x.shape
    depth = ffn1_norm.shape[0]
    cos, sin = _rope_tables(seq)
    h = x.reshape(batch * seq, D_MODEL)
    for layer in range(depth):
        h = _ffn_block(h, ffn1_norm[layer], ffn1_w_gate[layer].astype(BF16),
                       ffn1_w_up[layer].astype(BF16), ffn1_w_down[layer].astype(BF16))
        q_sb, kt_sb, v_sb, q_dl, k_dl, v_dl = _in_proj(
            h, mix_norm[layer], w_in[layer].astype(BF16), cos, sin, batch, seq)
        o_sb = _sb_attention(q_sb, kt_sb, v_sb, batch, seq)
        dl_parts = [_dilated_pattern(q_dl, k_dl, v_dl, batch, seq, window, dilation)
                    for window, dilation in DILATED_PATTERNS]
        h = _mix_out(h, o_sb, dl_parts, sb_out_norm[layer], dil_out_norm[layer], w_out[layer].astype(BF16))
        last = layer == depth - 1
        h = _ffn_block(h, ffn2_norm[layer], ffn2_w_gate[layer].astype(BF16),
                       ffn2_w_up[layer].astype(BF16), ffn2_w_down[layer].astype(BF16),
                       final_gain=final_norm if last else None)
    return h.reshape(batch, seq, D_MODEL)
```

```python
import functools
import math

import jax
import jax.numpy as jnp
from jax import lax
from jax.experimental import pallas as pl
from jax.experimental.pallas import tpu as pltpu

D_MODEL = 1024
HEAD_DIM = 64
D_GROUP = 512
D_IN = 6 * D_GROUP
D_FF = 2816
DILATED_PATTERNS = ((128, 1), (512, 4), (2048, 16))
ROPE_THETA = 10000.0
RMS_EPS = 1e-6
HALF_STEP = 0.5

LANES = 128
LOG2E = math.log2(math.e)
INV_LN2 = 1.0 / math.log(2.0)
Q_SCALE = HEAD_DIM ** -0.5 * LOG2E

TM = 512
FF_CHUNK = 256
SB_TQ = 512
SB_KB = 256
SB_DEAD_LOG2 = -160.0
SB_BLOCKS_PER_TILE = SB_TQ // SB_KB
assert SB_BLOCKS_PER_TILE == 2
DL_TQ = 128
DL_GROUP = 5
DL_SPAN = DL_TQ * max(d for _, d in DILATED_PATTERNS)
assert [d for _, d in DILATED_PATTERNS] == [1, 4, 16] and all(w // d == DL_TQ for w, d in DILATED_PATTERNS)
VMEM_LIMIT = 48 * 1024 * 1024

F32 = jnp.float32
BF16 = jnp.bfloat16


def _rms(x):
    return x * lax.rsqrt(jnp.mean(x * x, axis=-1, keepdims=True) + RMS_EPS)


def _params(*sem):
    return pltpu.CompilerParams(dimension_semantics=sem, vmem_limit_bytes=VMEM_LIMIT)


def _resident(shape):
    return pl.BlockSpec(shape, lambda *_: (0,) * len(shape), pipeline_mode=pl.Buffered(1))


def _ffn_kernel(x_ref, gain_ref, wg_ref, wu_ref, wd_ref, *rest, final_norm):
    o_ref = rest[-1]
    x = x_ref[...]
    h = (_rms(x) * gain_ref[...]).astype(BF16)
    acc = jnp.zeros(x.shape, F32)
    for c in range(0, D_FF, FF_CHUNK):
        g = jnp.dot(h, wg_ref[:, c:c + FF_CHUNK], preferred_element_type=F32)
        u = jnp.dot(h, wu_ref[:, c:c + FF_CHUNK], preferred_element_type=F32)
        a = (g * jax.nn.sigmoid(g) * u).astype(BF16)
        acc = acc + jnp.dot(a, wd_ref[c:c + FF_CHUNK, :], preferred_element_type=F32)
    y = x + HALF_STEP * acc
    if final_norm:
        y = _rms(y) * rest[0][...]
    o_ref[...] = y


def _ffn_block(x, gain, wg, wu, wd, final_gain=None):
    t = x.shape[0]
    row = pl.BlockSpec((TM, D_MODEL), lambda i: (i, 0))
    vec = _resident((1, D_MODEL))
    in_specs = [row, vec, _resident((D_MODEL, D_FF)), _resident((D_MODEL, D_FF)), _resident((D_FF, D_MODEL))]
    args = [x, gain.reshape(1, D_MODEL), wg, wu, wd]
    if final_gain is not None:
        in_specs.append(vec)
        args.append(final_gain.reshape(1, D_MODEL))
    return pl.pallas_call(
        functools.partial(_ffn_kernel, final_norm=final_gain is not None),
        grid=(t // TM,),
        in_specs=in_specs,
        out_specs=row,
        out_shape=jax.ShapeDtypeStruct((t, D_MODEL), F32),
        compiler_params=_params("parallel"),
        name="ffn_block",
    )(*args)


def _in_proj_kernel(x_ref, gain_ref, w_ref, cos_ref, sin_ref,
                    qsb_ref, ktsb_ref, vsb_ref, qdl_ref, kdl_ref, vdl_ref):
    h = (_rms(x_ref[...]) * gain_ref[...]).astype(BF16)

    def proj(j):
        return jnp.dot(h, w_ref[:, j * D_GROUP:(j + 1) * D_GROUP], preferred_element_type=F32)

    cos = cos_ref[...]
    sin = sin_ref[...]
    lane = lax.broadcasted_iota(jnp.int32, cos.shape, 1)
    low_half = (lane % HEAD_DIM) < HEAD_DIM // 2

    def rope(t):
        outs = []
        for j in range(D_GROUP // LANES):
            tj = t[:, j * LANES:(j + 1) * LANES]
            ahead = pltpu.roll(tj, LANES - HEAD_DIM // 2, 1)
            behind = pltpu.roll(tj, HEAD_DIM // 2, 1)
            outs.append(tj * cos + jnp.where(low_half, ahead, behind) * sin)
        return jnp.concatenate(outs, axis=1)

    qsb_ref[...] = (proj(0) * Q_SCALE).astype(BF16)
    kt = proj(1).T
    for hp in range(D_GROUP // LANES):
        for kb in range(TM // SB_KB):
            ktsb_ref[hp, kb] = kt[hp * LANES:(hp + 1) * LANES, kb * SB_KB:(kb + 1) * SB_KB].astype(BF16)
    vsb_ref[...] = proj(2).astype(BF16)
    qdl_ref[...] = (rope(proj(3)) * Q_SCALE).astype(BF16)
    kdl_ref[...] = rope(proj(4)).astype(BF16)
    vdl_ref[...] = proj(5).astype(BF16)


def _in_proj(x, gain, w_in, cos, sin, batch, seq):
    t = x.shape[0]
    n_seq_tiles = seq // TM
    row = lambda w: pl.BlockSpec((TM, w), lambda b, i: (b * n_seq_tiles + i, 0))
    grp = jax.ShapeDtypeStruct((t, D_GROUP), BF16)
    kt_shape = (batch, D_GROUP // LANES, seq // SB_KB, LANES, SB_KB)
    return pl.pallas_call(
        _in_proj_kernel,
        grid=(batch, n_seq_tiles),
        in_specs=[row(D_MODEL), _resident((1, D_MODEL)), _resident((D_MODEL, D_IN)),
                  pl.BlockSpec((TM, LANES), lambda b, i: (i, 0)),
                  pl.BlockSpec((TM, LANES), lambda b, i: (i, 0))],
        out_specs=[row(D_GROUP),
                   pl.BlockSpec((None, D_GROUP // LANES, TM // SB_KB, LANES, SB_KB),
                                lambda b, i: (b, 0, i, 0, 0)),
                   row(D_GROUP), row(D_GROUP), row(D_GROUP), row(D_GROUP)],
        out_shape=[grp, jax.ShapeDtypeStruct(kt_shape, BF16), grp, grp, grp, grp],
        compiler_params=_params("parallel", "parallel"),
        name="in_proj",
    )(x, gain.reshape(1, D_MODEL), w_in, cos, sin)


def _sb_kernel(q_ref, kt_ref, v_ref, csum_ref, o_ref, qs_ref, acc_ref, run_ref, w_ref, a_ref):
    i = pl.program_id(2)
    n_heads = LANES // HEAD_DIM
    q = q_ref[...]
    lane = lax.broadcasted_iota(jnp.int32, (SB_TQ, LANES), 1)
    for hh in range(n_heads):
        in_head = (lane >= hh * HEAD_DIM) & (lane < (hh + 1) * HEAD_DIM)
        qs_ref[hh * SB_TQ:(hh + 1) * SB_TQ, :] = jnp.where(in_head, q, jnp.zeros_like(q))
    run_ref[...] = jnp.zeros_like(run_ref)
    acc_ref[...] = jnp.zeros_like(acc_ref)
    a_ref[1] = jnp.zeros(a_ref.shape[1:], a_ref.dtype)
    csum_w = csum_ref[...]

    def scores(kb):
        return jnp.dot(qs_ref[...], kt_ref[kb], preferred_element_type=F32)

    def weights(w, mask):
        p = jnp.maximum(w, 0.0) + jnp.log(1.0 + jnp.exp2(-jnp.abs(w))) * INV_LN2
        if mask is not None:
            p = jnp.where(mask, p, 0.0)
        neg_c = jnp.dot(p.astype(BF16), csum_w, preferred_element_type=F32)
        run = run_ref[...]
        a = jnp.exp2((w - p) + neg_c + jnp.concatenate([run] * (SB_KB // LANES), axis=1))
        if mask is not None:
            a = jnp.where(mask, a, 0.0)
        run_ref[...] = run + jnp.broadcast_to(neg_c[:, 0:1] - p[:, 0:1], run.shape)
        return a.astype(BF16)

    def accumulate(a, kb):
        v = v_ref[pl.ds(pl.multiple_of(kb * SB_KB, SB_KB), SB_KB), :]
        acc_ref[...] += jnp.dot(a, v, preferred_element_type=F32)

    first_old = i * SB_BLOCKS_PER_TILE - 1
    w_ref[0] = scores(jnp.maximum(first_old, 0))

    row = lax.broadcasted_iota(jnp.int32, (n_heads * SB_TQ, SB_KB), 0) % SB_TQ
    col = lax.broadcasted_iota(jnp.int32, (n_heads * SB_TQ, SB_KB), 1)
    for d in range(SB_BLOCKS_PER_TILE):
        kb_in_tile = SB_BLOCKS_PER_TILE - 1 - d
        strictly_before = col + kb_in_tile * SB_KB < row
        kb = i * SB_BLOCKS_PER_TILE + kb_in_tile
        accumulate(weights(scores(kb), strictly_before), kb)

    def step(kb, s):
        w_ref[1 - s] = scores(jnp.maximum(kb - 1, 0))
        accumulate(a_ref[1 - s], kb + 1)
        a_ref[s] = weights(w_ref[s], None)

    def two_steps(carry):
        u, _ = carry
        kb = first_old - 2 * u
        step(kb, 0)
        alive = (jnp.max(run_ref[...]) > SB_DEAD_LOG2).astype(jnp.int32)
        step(kb - 1, 1)
        return u + 1, alive

    n_pairs, _ = lax.while_loop(lambda c: (c[0] < i) & (c[1] > 0), two_steps, (jnp.int32(0), jnp.int32(1)))
    accumulate(a_ref[1], first_old + 1 - 2 * n_pairs)
    o_ref[...] = jnp.where(lane < HEAD_DIM, acc_ref[:SB_TQ, :], acc_ref[SB_TQ:, :])


def _csum_weights():
    j = jnp.arange(SB_KB)[:, None]
    s = jnp.arange(SB_KB)[None, :]
    return jnp.where(j > s, -1.0, 0.0).astype(BF16)


def _sb_attention(q, kt, v, batch, seq):
    t = q.shape[0]
    n_q = seq // SB_TQ
    n_hp = D_GROUP // LANES
    return pl.pallas_call(
        _sb_kernel,
        grid=(batch, n_hp, n_q),
        in_specs=[pl.BlockSpec((SB_TQ, LANES), lambda b, hp, i: (b * n_q + i, hp)),
                  pl.BlockSpec((None, None, seq // SB_KB, LANES, SB_KB), lambda b, hp, i: (b, hp, 0, 0, 0)),
                  pl.BlockSpec((seq, LANES), lambda b, hp, i: (b, hp)),
                  _resident((SB_KB, SB_KB))],
        out_specs=pl.BlockSpec((SB_TQ, LANES), lambda b, hp, i: (b * n_q + i, hp)),
        out_shape=jax.ShapeDtypeStruct((t, D_GROUP), F32),
        scratch_shapes=[pltpu.VMEM((2 * SB_TQ, LANES), BF16),
                        pltpu.VMEM((2 * SB_TQ, LANES), F32),
                        pltpu.VMEM((2 * SB_TQ, LANES), F32),
                        pltpu.VMEM((2, 2 * SB_TQ, SB_KB), F32),
                        pltpu.VMEM((2, 2 * SB_TQ, SB_KB), BF16)],
        compiler_params=_params("parallel", "parallel", "arbitrary"),
        name="sb_attention",
    )(q, kt, v, _csum_weights())


def _dilated_kernel(q_ref, kp_ref, kc_ref, vp_ref, vc_ref, o_ref,
                    qn_ref, kn_ref, vn_ref, on4_ref, ln4_ref, on16_ref, ln16_ref):
    st = pl.program_id(2)
    n_heads = LANES // HEAD_DIM
    lane = lax.broadcasted_iota(jnp.int32, (DL_TQ, LANES), 1)
    row = lax.broadcasted_iota(jnp.int32, (DL_TQ, 2 * DL_TQ), 0)
    col = lax.broadcasted_iota(jnp.int32, (DL_TQ, 2 * DL_TQ), 1)
    band = (col >= row) & (col <= row + DL_TQ)
    band_edge = band & (col >= jnp.where(st == 0, DL_TQ, 0))
    head_lanes = [(lane >= hh * HEAD_DIM) & (lane < (hh + 1) * HEAD_DIM) for hh in range(n_heads)]

    qn_ref[...] = q_ref[...].astype(F32)
    kn_ref[:DL_SPAN, :] = kp_ref[...].astype(F32)
    kn_ref[DL_SPAN:, :] = kc_ref[...].astype(F32)
    vn_ref[:DL_SPAN, :] = vp_ref[...].astype(F32)
    vn_ref[DL_SPAN:, :] = vc_ref[...].astype(F32)

    def attend(q, k, v, valid):
        outs, lses = [], []
        for hh in range(n_heads):
            qh = jnp.where(head_lanes[hh], q, jnp.zeros_like(q))
            z = lax.dot_general(qh, k, (((1,), (1,)), ((), ())), preferred_element_type=F32)
            z = jnp.where(valid, z, -jnp.inf)
            m = jnp.max(z, axis=-1, keepdims=True)
            p = jnp.exp2(z - m)
            denom = jnp.sum(p, axis=-1, keepdims=True)
            outs.append(jnp.dot(p.astype(BF16), v, preferred_element_type=F32) / denom)
            lses.append(jnp.broadcast_to(m + jnp.log(denom) * INV_LN2, (DL_TQ, LANES)))
        return (jnp.where(lane < HEAD_DIM, outs[0], outs[1]), jnp.where(lane < HEAD_DIM, lses[0], lses[1]))

    def strided_pattern(d, on_ref, ln_ref):
        tiles_per_class = DL_SPAN // (d * DL_TQ)

        def one_class(c, carry):
            for j in range(tiles_per_class):
                q0 = c + d * DL_TQ * j
                k0 = DL_SPAN + q0 - d * DL_TQ
                q = qn_ref[pl.ds(q0, DL_TQ, stride=d), :].astype(BF16)
                k = kn_ref[pl.ds(k0, 2 * DL_TQ, stride=d), :].astype(BF16)
                v = vn_ref[pl.ds(k0, 2 * DL_TQ, stride=d), :].astype(BF16)
                o, lse = attend(q, k, v, band_edge if j == 0 else band)
                on_ref[pl.ds(q0, DL_TQ, stride=d), :] = o
                ln_ref[pl.ds(q0, DL_TQ, stride=d), :] = lse
            return carry

        lax.fori_loop(0, d, one_class, 0, unroll=DL_SPAN // (tiles_per_class * DL_TQ * 2))

    strided_pattern(16, on16_ref, ln16_ref)
    strided_pattern(4, on4_ref, ln4_ref)

    def mix(o1, l1, r0):
        o4, l4 = on4_ref[pl.ds(r0, DL_TQ), :], ln4_ref[pl.ds(r0, DL_TQ), :]
        o16, l16 = on16_ref[pl.ds(r0, DL_TQ), :], ln16_ref[pl.ds(r0, DL_TQ), :]
        top = jnp.maximum(jnp.maximum(l1, l4), l16)
        e1, e4, e16 = jnp.exp2(l1 - top), jnp.exp2(l4 - top), jnp.exp2(l16 - top)
        return (e1 * o1 + e4 * o4 + e16 * o16) / (e1 + e4 + e16)

    k_edge = jnp.concatenate([kp_ref[DL_SPAN - DL_TQ:, :], kc_ref[:DL_TQ, :]], axis=0)
    v_edge = jnp.concatenate([vp_ref[DL_SPAN - DL_TQ:, :], vc_ref[:DL_TQ, :]], axis=0)
    o1, l1 = attend(q_ref[:DL_TQ, :], k_edge, v_edge, band_edge)
    o_ref[:DL_TQ, :] = mix(o1, l1, 0)

    def dense_group(g, carry):
        for u in range(DL_GROUP):
            k0 = pl.multiple_of((g * DL_GROUP + u) * DL_TQ, DL_TQ)
            q0 = k0 + DL_TQ
            o1, l1 = attend(q_ref[pl.ds(q0, DL_TQ), :], kc_ref[pl.ds(k0, 2 * DL_TQ), :],
                            vc_ref[pl.ds(k0, 2 * DL_TQ), :], band)
            o_ref[pl.ds(q0, DL_TQ), :] = mix(o1, l1, q0)
        return carry

    lax.fori_loop(0, (DL_SPAN // DL_TQ - 1) // DL_GROUP, dense_group, 0)


def _dilated_attention(q, k, v, batch, seq):
    t = q.shape[0]
    assert seq % DL_SPAN == 0 and (DL_SPAN // DL_TQ - 1) % DL_GROUP == 0
    n_span = seq // DL_SPAN
    cur = pl.BlockSpec((DL_SPAN, LANES), lambda b, hp, st: (b * n_span + st, hp))
    prev = pl.BlockSpec((DL_SPAN, LANES), lambda b, hp, st: (b * n_span + jnp.maximum(st - 1, 0), hp))
    nat = pltpu.VMEM((DL_SPAN, LANES), F32)
    return pl.pallas_call(
        _dilated_kernel,
        grid=(batch, D_GROUP // LANES, n_span),
        in_specs=[cur, prev, cur, prev, cur],
        out_specs=cur,
        out_shape=jax.ShapeDtypeStruct((t, D_GROUP), F32),
        scratch_shapes=[nat, pltpu.VMEM((2 * DL_SPAN, LANES), F32), pltpu.VMEM((2 * DL_SPAN, LANES), F32),
                        nat, nat, nat, nat],
        compiler_params=_params("parallel", "parallel", "arbitrary"),
        name="dilated_mixture",
    )(q, k, k, v, v)


def _mix_out_kernel(x_ref, osb_ref, odl_ref, gsb_ref, gdl_ref, w_ref, y_ref):
    sb = (_rms(osb_ref[...]) * gsb_ref[...]).astype(BF16)
    dl = (_rms(odl_ref[...]) * gdl_ref[...]).astype(BF16)
    y = jnp.dot(sb, w_ref[:D_GROUP, :], preferred_element_type=F32)
    y = y + jnp.dot(dl, w_ref[D_GROUP:, :], preferred_element_type=F32)
    y_ref[...] = x_ref[...] + y


def _mix_out(x, o_sb, o_dl, g_sb, g_dl, w_out):
    t = x.shape[0]
    row = lambda w: pl.BlockSpec((TM, w), lambda i: (i, 0))
    return pl.pallas_call(
        _mix_out_kernel,
        grid=(t // TM,),
        in_specs=[row(D_MODEL), row(D_GROUP), row(D_GROUP),
                  _resident((1, D_GROUP)), _resident((1, D_GROUP)), _resident((D_MODEL, D_MODEL))],
        out_specs=row(D_MODEL),
        out_shape=jax.ShapeDtypeStruct((t, D_MODEL), F32),
        compiler_params=_params("parallel"),
        name="mix_out",
    )(x, o_sb, o_dl, g_sb.reshape(1, D_GROUP), g_dl.reshape(1, D_GROUP), w_out)


def _rope_tables(seq):
    half = HEAD_DIM // 2
    inv_freq = ROPE_THETA ** (-jnp.arange(half, dtype=F32) / half)
    ang = jnp.arange(seq).astype(F32)[:, None] * inv_freq[None, :]
    cos, sin = jnp.cos(ang), jnp.sin(ang)
    reps = LANES // HEAD_DIM
    cos_t = jnp.tile(jnp.concatenate([cos, cos], axis=1), (1, reps))
    sin_t = jnp.tile(jnp.concatenate([-sin, sin], axis=1), (1, reps))
    return cos_t, sin_t


def kernel(x, ffn1_norm, ffn1_w_gate, ffn1_w_up, ffn1_w_down, mix_norm, w_in, sb_out_norm, dil_out_norm, w_out, ffn2_norm, ffn2_w_gate, ffn2_w_up, ffn2_w_down, final_norm):
    batch, seq, _ = x.shape
    depth = ffn1_norm.shape[0]
    cos, sin = _rope_tables(seq)
    h = x.reshape(batch * seq, D_MODEL)
    for layer in range(depth):
        h = _ffn_block(h, ffn1_norm[layer], ffn1_w_gate[layer].astype(BF16),
                       ffn1_w_up[layer].astype(BF16), ffn1_w_down[layer].astype(BF16))
        q_sb, kt_sb, v_sb, q_dl, k_dl, v_dl = _in_proj(
            h, mix_norm[layer], w_in[layer].astype(BF16), cos, sin, batch, seq)
        o_sb = _sb_attention(q_sb, kt_sb, v_sb, batch, seq)
        o_dl = _dilated_attention(q_dl, k_dl, v_dl, batch, seq)
        h = _mix_out(h, o_sb, o_dl, sb_out_norm[layer], dil_out_norm[layer], w_out[layer].astype(BF16))
        last = layer == depth - 1
        h = _ffn_block(h, ffn2_norm[layer], ffn2_w_gate[layer].astype(BF16),
                       ffn2_w_up[layer].astype(BF16), ffn2_w_down[layer].astype(BF16),
                       final_gain=final_norm if last else None)
    return h.reshape(batch, seq, D_MODEL)
```

```python
import functools
import math

import jax
import jax.numpy as jnp
from jax import lax
from jax.experimental import pallas as pl
from jax.experimental.pallas import tpu as pltpu

D_MODEL = 1024
HEAD_DIM = 64
D_GROUP = 512
D_IN = 6 * D_GROUP
D_FF = 2816
DILATED_PATTERNS = ((128, 1), (512, 4), (2048, 16))
ROPE_THETA = 10000.0
RMS_EPS = 1e-6
HALF_STEP = 0.5

LANES = 128
LOG2E = math.log2(math.e)
INV_LN2 = 1.0 / math.log(2.0)
Q_SCALE = HEAD_DIM ** -0.5 * LOG2E

TM = 512
FF_CHUNK = 256
SB_TQ = 256
SB_PAIRS = 4
SB_KB = 256
SB_DEAD_LOG2 = -160.0
assert SB_TQ == SB_KB
DL_TQ = 128
DL_GROUP = 5
DL_SPAN = DL_TQ * max(d for _, d in DILATED_PATTERNS)
assert [d for _, d in DILATED_PATTERNS] == [1, 4, 16] and all(w // d == DL_TQ for w, d in DILATED_PATTERNS)
VMEM_LIMIT = 48 * 1024 * 1024

F32 = jnp.float32
BF16 = jnp.bfloat16


def _rms(x):
    return x * lax.rsqrt(jnp.mean(x * x, axis=-1, keepdims=True) + RMS_EPS)


def _params(*sem):
    return pltpu.CompilerParams(dimension_semantics=sem, vmem_limit_bytes=VMEM_LIMIT)


def _resident(shape):
    return pl.BlockSpec(shape, lambda *_: (0,) * len(shape), pipeline_mode=pl.Buffered(1))


def _ffn_kernel(x_ref, gain_ref, wg_ref, wu_ref, wd_ref, *rest, final_norm):
    o_ref = rest[-1]
    x = x_ref[...]
    h = (_rms(x) * gain_ref[...]).astype(BF16)
    acc = jnp.zeros(x.shape, F32)
    for c in range(0, D_FF, FF_CHUNK):
        g = jnp.dot(h, wg_ref[:, c:c + FF_CHUNK], preferred_element_type=F32)
        u = jnp.dot(h, wu_ref[:, c:c + FF_CHUNK], preferred_element_type=F32)
        a = (g * jax.nn.sigmoid(g) * u).astype(BF16)
        acc = acc + jnp.dot(a, wd_ref[c:c + FF_CHUNK, :], preferred_element_type=F32)
    y = x + HALF_STEP * acc
    if final_norm:
        y = _rms(y) * rest[0][...]
    o_ref[...] = y


def _ffn_block(x, gain, wg, wu, wd, final_gain=None):
    t = x.shape[0]
    row = pl.BlockSpec((TM, D_MODEL), lambda i: (i, 0))
    vec = _resident((1, D_MODEL))
    in_specs = [row, vec, _resident((D_MODEL, D_FF)), _resident((D_MODEL, D_FF)), _resident((D_FF, D_MODEL))]
    args = [x, gain.reshape(1, D_MODEL), wg, wu, wd]
    if final_gain is not None:
        in_specs.append(vec)
        args.append(final_gain.reshape(1, D_MODEL))
    return pl.pallas_call(
        functools.partial(_ffn_kernel, final_norm=final_gain is not None),
        grid=(t // TM,),
        in_specs=in_specs,
        out_specs=row,
        out_shape=jax.ShapeDtypeStruct((t, D_MODEL), F32),
        compiler_params=_params("parallel"),
        name="ffn_block",
    )(*args)


def _in_proj_kernel(x_ref, gain_ref, w_ref, cos_ref, sin_ref,
                    qsb_ref, ktsb_ref, vsb_ref, qdl_ref, kdl_ref, vdl_ref):
    h = (_rms(x_ref[...]) * gain_ref[...]).astype(BF16)

    def proj(j):
        return jnp.dot(h, w_ref[:, j * D_GROUP:(j + 1) * D_GROUP], preferred_element_type=F32)

    cos = cos_ref[...]
    sin = sin_ref[...]
    lane = lax.broadcasted_iota(jnp.int32, cos.shape, 1)
    low_half = (lane % HEAD_DIM) < HEAD_DIM // 2

    def rope(t):
        outs = []
        for j in range(D_GROUP // LANES):
            tj = t[:, j * LANES:(j + 1) * LANES]
            ahead = pltpu.roll(tj, LANES - HEAD_DIM // 2, 1)
            behind = pltpu.roll(tj, HEAD_DIM // 2, 1)
            outs.append(tj * cos + jnp.where(low_half, ahead, behind) * sin)
        return jnp.concatenate(outs, axis=1)

    qsb_ref[...] = (proj(0) * Q_SCALE).astype(BF16)
    kt = proj(1).T
    for hp in range(D_GROUP // LANES):
        for kb in range(TM // SB_KB):
            ktsb_ref[hp, kb] = kt[hp * LANES:(hp + 1) * LANES, kb * SB_KB:(kb + 1) * SB_KB].astype(BF16)
    vsb_ref[...] = proj(2).astype(BF16)
    qdl_ref[...] = (rope(proj(3)) * Q_SCALE).astype(BF16)
    kdl_ref[...] = rope(proj(4)).astype(BF16)
    vdl_ref[...] = proj(5).astype(BF16)


def _in_proj(x, gain, w_in, cos, sin, batch, seq):
    t = x.shape[0]
    n_seq_tiles = seq // TM
    row = lambda w: pl.BlockSpec((TM, w), lambda b, i: (b * n_seq_tiles + i, 0))
    grp = jax.ShapeDtypeStruct((t, D_GROUP), BF16)
    kt_shape = (batch, D_GROUP // LANES, seq // SB_KB, LANES, SB_KB)
    return pl.pallas_call(
        _in_proj_kernel,
        grid=(batch, n_seq_tiles),
        in_specs=[row(D_MODEL), _resident((1, D_MODEL)), _resident((D_MODEL, D_IN)),
                  pl.BlockSpec((TM, LANES), lambda b, i: (i, 0)),
                  pl.BlockSpec((TM, LANES), lambda b, i: (i, 0))],
        out_specs=[row(D_GROUP),
                   pl.BlockSpec((None, D_GROUP // LANES, TM // SB_KB, LANES, SB_KB),
                                lambda b, i: (b, 0, i, 0, 0)),
                   row(D_GROUP), row(D_GROUP), row(D_GROUP), row(D_GROUP)],
        out_shape=[grp, jax.ShapeDtypeStruct(kt_shape, BF16), grp, grp, grp, grp],
        compiler_params=_params("parallel", "parallel"),
        name="in_proj",
    )(x, gain.reshape(1, D_MODEL), w_in, cos, sin)


def _sb_kernel(q_ref, kt_ref, v_ref, csum_ref, o_ref, qs_ref, acc_ref, run_ref):
    i = pl.program_id(2)
    n_heads = LANES // HEAD_DIM
    lane = lax.broadcasted_iota(jnp.int32, (SB_TQ, LANES), 1)
    for hp in range(SB_PAIRS):
        q = q_ref[:, hp * LANES:(hp + 1) * LANES]
        for hh in range(n_heads):
            in_head = (lane >= hh * HEAD_DIM) & (lane < (hh + 1) * HEAD_DIM)
            qs_ref[hp, hh * SB_TQ:(hh + 1) * SB_TQ, :] = jnp.where(in_head, q, jnp.zeros_like(q))
    run_ref[...] = jnp.zeros_like(run_ref)
    acc_ref[...] = jnp.zeros_like(acc_ref)
    csum_w = csum_ref[...]

    def scores(hp, kb):
        return jnp.dot(qs_ref[hp], kt_ref[hp, kb], preferred_element_type=F32)

    def weights(hp, w, mask):
        p = jnp.maximum(w, 0.0) + jnp.log(1.0 + jnp.exp2(-jnp.abs(w))) * INV_LN2
        if mask is not None:
            p = jnp.where(mask, p, 0.0)
        neg_c = jnp.dot(p.astype(BF16), csum_w, preferred_element_type=F32)
        run = run_ref[hp]
        a = jnp.exp2((w - p) + neg_c + jnp.concatenate([run] * (SB_KB // LANES), axis=1))
        if mask is not None:
            a = jnp.where(mask, a, 0.0)
        run_ref[hp] = run + jnp.broadcast_to(neg_c[:, 0:1] - p[:, 0:1], run.shape)
        return a.astype(BF16)

    def accumulate(hp, a, kb):
        v = v_ref[pl.ds(pl.multiple_of(kb * SB_KB, SB_KB), SB_KB), hp * LANES:(hp + 1) * LANES]
        acc_ref[hp] += jnp.dot(a, v, preferred_element_type=F32)

    def block(kb, mask):
        for hp in range(SB_PAIRS):
            accumulate(hp, weights(hp, scores(hp, kb), mask), kb)

    row = lax.broadcasted_iota(jnp.int32, (n_heads * SB_TQ, SB_KB), 0) % SB_TQ
    col = lax.broadcasted_iota(jnp.int32, (n_heads * SB_TQ, SB_KB), 1)
    block(i, col < row)

    run_ref[...] = jnp.where(i > 0, run_ref[...], 2 * SB_DEAD_LOG2)
    block(jnp.maximum(i - 1, 0), None)

    def alive():
        return (jnp.max(run_ref[...]) > SB_DEAD_LOG2).astype(jnp.int32)

    def older(carry):
        kb, _ = carry
        block(kb, None)
        return kb - 1, alive()

    lax.while_loop(lambda c: (c[0] >= 0) & (c[1] > 0), older, (i - 2, alive()))
    for hp in range(SB_PAIRS):
        o_ref[:, hp * LANES:(hp + 1) * LANES] = jnp.where(lane < HEAD_DIM, acc_ref[hp, :SB_TQ, :], acc_ref[hp, SB_TQ:, :])


def _csum_weights():
    j = jnp.arange(SB_KB)[:, None]
    s = jnp.arange(SB_KB)[None, :]
    return jnp.where(j > s, -1.0, 0.0).astype(BF16)


def _sb_attention(q, kt, v, batch, seq):
    t = q.shape[0]
    n_q = seq // SB_TQ
    n_groups = D_GROUP // (SB_PAIRS * LANES)
    m = (LANES // HEAD_DIM) * SB_TQ
    return pl.pallas_call(
        _sb_kernel,
        grid=(batch, n_groups, n_q),
        in_specs=[pl.BlockSpec((SB_TQ, SB_PAIRS * LANES), lambda b, g, i: (b * n_q + i, g)),
                  pl.BlockSpec((None, SB_PAIRS, seq // SB_KB, LANES, SB_KB), lambda b, g, i: (b, g, 0, 0, 0)),
                  pl.BlockSpec((seq, SB_PAIRS * LANES), lambda b, g, i: (b, g)),
                  _resident((SB_KB, SB_KB))],
        out_specs=pl.BlockSpec((SB_TQ, SB_PAIRS * LANES), lambda b, g, i: (b * n_q + i, g)),
        out_shape=jax.ShapeDtypeStruct((t, D_GROUP), F32),
        scratch_shapes=[pltpu.VMEM((SB_PAIRS, m, LANES), BF16),
                        pltpu.VMEM((SB_PAIRS, m, LANES), F32),
                        pltpu.VMEM((SB_PAIRS, m, LANES), F32)],
        compiler_params=_params("parallel", "parallel", "arbitrary"),
        name="sb_attention",
    )(q, kt, v, _csum_weights())


def _dilated_kernel(q_ref, kp_ref, kc_ref, vp_ref, vc_ref, o_ref,
                    qn_ref, kn_ref, vn_ref, on4_ref, ln4_ref, on16_ref, ln16_ref):
    st = pl.program_id(2)
    n_heads = LANES // HEAD_DIM
    lane = lax.broadcasted_iota(jnp.int32, (DL_TQ, LANES), 1)
    row = lax.broadcasted_iota(jnp.int32, (DL_TQ, 2 * DL_TQ), 0)
    col = lax.broadcasted_iota(jnp.int32, (DL_TQ, 2 * DL_TQ), 1)
    band = (col >= row) & (col <= row + DL_TQ)
    band_edge = band & (col >= jnp.where(st == 0, DL_TQ, 0))
    head_lanes = [(lane >= hh * HEAD_DIM) & (lane < (hh + 1) * HEAD_DIM) for hh in range(n_heads)]

    qn_ref[...] = q_ref[...].astype(F32)
    kn_ref[:DL_SPAN, :] = kp_ref[...].astype(F32)
    kn_ref[DL_SPAN:, :] = kc_ref[...].astype(F32)
    vn_ref[:DL_SPAN, :] = vp_ref[...].astype(F32)
    vn_ref[DL_SPAN:, :] = vc_ref[...].astype(F32)

    def attend(q, k, v, valid):
        outs, lses = [], []
        for hh in range(n_heads):
            qh = jnp.where(head_lanes[hh], q, jnp.zeros_like(q))
            z = lax.dot_general(qh, k, (((1,), (1,)), ((), ())), preferred_element_type=F32)
            z = jnp.where(valid, z, -jnp.inf)
            m = jnp.max(z, axis=-1, keepdims=True)
            p = jnp.exp2(z - m)
            denom = jnp.sum(p, axis=-1, keepdims=True)
            outs.append(jnp.dot(p.astype(BF16), v, preferred_element_type=F32) / denom)
            lses.append(jnp.broadcast_to(m + jnp.log(denom) * INV_LN2, (DL_TQ, LANES)))
        return (jnp.where(lane < HEAD_DIM, outs[0], outs[1]), jnp.where(lane < HEAD_DIM, lses[0], lses[1]))

    def strided_pattern(d, on_ref, ln_ref):
        tiles_per_class = DL_SPAN // (d * DL_TQ)

        def one_class(c, carry):
            for j in range(tiles_per_class):
                q0 = c + d * DL_TQ * j
                k0 = DL_SPAN + q0 - d * DL_TQ
                q = qn_ref[pl.ds(q0, DL_TQ, stride=d), :].astype(BF16)
                k = kn_ref[pl.ds(k0, 2 * DL_TQ, stride=d), :].astype(BF16)
                v = vn_ref[pl.ds(k0, 2 * DL_TQ, stride=d), :].astype(BF16)
                o, lse = attend(q, k, v, band_edge if j == 0 else band)
                on_ref[pl.ds(q0, DL_TQ, stride=d), :] = o
                ln_ref[pl.ds(q0, DL_TQ, stride=d), :] = lse
            return carry

        lax.fori_loop(0, d, one_class, 0, unroll=DL_SPAN // (tiles_per_class * DL_TQ * 2))

    strided_pattern(16, on16_ref, ln16_ref)
    strided_pattern(4, on4_ref, ln4_ref)

    def mix(o1, l1, r0):
        o4, l4 = on4_ref[pl.ds(r0, DL_TQ), :], ln4_ref[pl.ds(r0, DL_TQ), :]
        o16, l16 = on16_ref[pl.ds(r0, DL_TQ), :], ln16_ref[pl.ds(r0, DL_TQ), :]
        top = jnp.maximum(jnp.maximum(l1, l4), l16)
        e1, e4, e16 = jnp.exp2(l1 - top), jnp.exp2(l4 - top), jnp.exp2(l16 - top)
        return (e1 * o1 + e4 * o4 + e16 * o16) / (e1 + e4 + e16)

    k_edge = jnp.concatenate([kp_ref[DL_SPAN - DL_TQ:, :], kc_ref[:DL_TQ, :]], axis=0)
    v_edge = jnp.concatenate([vp_ref[DL_SPAN - DL_TQ:, :], vc_ref[:DL_TQ, :]], axis=0)
    o1, l1 = attend(q_ref[:DL_TQ, :], k_edge, v_edge, band_edge)
    o_ref[:DL_TQ, :] = mix(o1, l1, 0)

    def dense_group(g, carry):
        for u in range(DL_GROUP):
            k0 = pl.multiple_of((g * DL_GROUP + u) * DL_TQ, DL_TQ)
            q0 = k0 + DL_TQ
            o1, l1 = attend(q_ref[pl.ds(q0, DL_TQ), :], kc_ref[pl.ds(k0, 2 * DL_TQ), :],
                            vc_ref[pl.ds(k0, 2 * DL_TQ), :], band)
            o_ref[pl.ds(q0, DL_TQ), :] = mix(o1, l1, q0)
        return carry

    lax.fori_loop(0, (DL_SPAN // DL_TQ - 1) // DL_GROUP, dense_group, 0)


def _dilated_attention(q, k, v, batch, seq):
    t = q.shape[0]
    assert seq % DL_SPAN == 0 and (DL_SPAN // DL_TQ - 1) % DL_GROUP == 0
    n_span = seq // DL_SPAN
    cur = pl.BlockSpec((DL_SPAN, LANES), lambda b, hp, st: (b * n_span + st, hp))
    prev = pl.BlockSpec((DL_SPAN, LANES), lambda b, hp, st: (b * n_span + jnp.maximum(st - 1, 0), hp))
    nat = pltpu.VMEM((DL_SPAN, LANES), F32)
    return pl.pallas_call(
        _dilated_kernel,
        grid=(batch, D_GROUP // LANES, n_span),
        in_specs=[cur, prev, cur, prev, cur],
        out_specs=cur,
        out_shape=jax.ShapeDtypeStruct((t, D_GROUP), F32),
        scratch_shapes=[nat, pltpu.VMEM((2 * DL_SPAN, LANES), F32), pltpu.VMEM((2 * DL_SPAN, LANES), F32),
                        nat, nat, nat, nat],
        compiler_params=_params("parallel", "parallel", "arbitrary"),
        name="dilated_mixture",
    )(q, k, k, v, v)


def _mix_out_kernel(x_ref, osb_ref, odl_ref, gsb_ref, gdl_ref, w_ref, y_ref):
    sb = (_rms(osb_ref[...]) * gsb_ref[...]).astype(BF16)
    dl = (_rms(odl_ref[...]) * gdl_ref[...]).astype(BF16)
    y = jnp.dot(sb, w_ref[:D_GROUP, :], preferred_element_type=F32)
    y = y + jnp.dot(dl, w_ref[D_GROUP:, :], preferred_element_type=F32)
    y_ref[...] = x_ref[...] + y


def _mix_out(x, o_sb, o_dl, g_sb, g_dl, w_out):
    t = x.shape[0]
    row = lambda w: pl.BlockSpec((TM, w), lambda i: (i, 0))
    return pl.pallas_call(
        _mix_out_kernel,
        grid=(t // TM,),
        in_specs=[row(D_MODEL), row(D_GROUP), row(D_GROUP),
                  _resident((1, D_GROUP)), _resident((1, D_GROUP)), _resident((D_MODEL, D_MODEL))],
        out_specs=row(D_MODEL),
        out_shape=jax.ShapeDtypeStruct((t, D_MODEL), F32),
        compiler_params=_params("parallel"),
        name="mix_out",
    )(x, o_sb, o_dl, g_sb.reshape(1, D_GROUP), g_dl.reshape(1, D_GROUP), w_out)


def _rope_tables(seq):
    half = HEAD_DIM // 2
    inv_freq = ROPE_THETA ** (-jnp.arange(half, dtype=F32) / half)
    ang = jnp.arange(seq).astype(F32)[:, None] * inv_freq[None, :]
    cos, sin = jnp.cos(ang), jnp.sin(ang)
    reps = LANES // HEAD_DIM
    cos_t = jnp.tile(jnp.concatenate([cos, cos], axis=1), (1, reps))
    sin_t = jnp.tile(jnp.concatenate([-sin, sin], axis=1), (1, reps))
    return cos_t, sin_t


def kernel(x, ffn1_norm, ffn1_w_gate, ffn1_w_up, ffn1_w_down, mix_norm, w_in, sb_out_norm, dil_out_norm, w_out, ffn2_norm, ffn2_w_gate, ffn2_w_up, ffn2_w_down, final_norm):
    batch, seq, _ = x.shape
    depth = ffn1_norm.shape[0]
    cos, sin = _rope_tables(seq)
    h = x.reshape(batch * seq, D_MODEL)
    for layer in range(depth):
        h = _ffn_block(h, ffn1_norm[layer], ffn1_w_gate[layer].astype(BF16),
                       ffn1_w_up[layer].astype(BF16), ffn1_w_down[layer].astype(BF16))
        q_sb, kt_sb, v_sb, q_dl, k_dl, v_dl = _in_proj(
            h, mix_norm[layer], w_in[layer].astype(BF16), cos, sin, batch, seq)
        o_sb = _sb_attention(q_sb, kt_sb, v_sb, batch, seq)
        o_dl = _dilated_attention(q_dl, k_dl, v_dl, batch, seq)
        h = _mix_out(h, o_sb, o_dl, sb_out_norm[layer], dil_out_norm[layer], w_out[layer].astype(BF16))
        last = layer == depth - 1
        h = _ffn_block(h, ffn2_norm[layer], ffn2_w_gate[layer].astype(BF16),
                       ffn2_w_up[layer].astype(BF16), ffn2_w_down[layer].astype(BF16),
                       final_gain=final_norm if last else None)
    return h.reshape(batch, seq, D_MODEL)
```

```python
import functools
import math

import jax
import jax.numpy as jnp
from jax import lax
from jax.experimental import pallas as pl
from jax.experimental.pallas import tpu as pltpu

D_MODEL = 1024
HEAD_DIM = 64
D_GROUP = 512
D_IN = 6 * D_GROUP
D_FF = 2816
DILATED_PATTERNS = ((128, 1), (512, 4), (2048, 16))
ROPE_THETA = 10000.0
RMS_EPS = 1e-6
HALF_STEP = 0.5

LANES = 128
LOG2E = math.log2(math.e)
INV_LN2 = 1.0 / math.log(2.0)
Q_SCALE = HEAD_DIM ** -0.5 * LOG2E

TM = 512
FF_CHUNK = 256
SB_TQ = 256
SB_PAIRS = 4
SB_KB = 256
SB_DEAD_LOG2 = -160.0
assert SB_TQ == SB_KB
DL_TQ = 128
DL_SPAN = DL_TQ * max(d for _, d in DILATED_PATTERNS)
assert [d for _, d in DILATED_PATTERNS] == [1, 4, 16] and all(w // d == DL_TQ for w, d in DILATED_PATTERNS)
VMEM_LIMIT = 48 * 1024 * 1024

F32 = jnp.float32
BF16 = jnp.bfloat16


def _rms(x):
    return x * lax.rsqrt(jnp.mean(x * x, axis=-1, keepdims=True) + RMS_EPS)


def _params(*sem):
    return pltpu.CompilerParams(dimension_semantics=sem, vmem_limit_bytes=VMEM_LIMIT)


def _resident(shape):
    return pl.BlockSpec(shape, lambda *_: (0,) * len(shape), pipeline_mode=pl.Buffered(1))


def _ffn_kernel(*refs, mixer_out, final_norm):
    refs = list(refs)
    o_ref = refs.pop()
    x = refs.pop(0)[...]
    if mixer_out:
        osb_ref, odl_ref, gsb_ref, gdl_ref, wout_ref = refs[:5]
        del refs[:5]
        sb = (_rms(osb_ref[...]) * gsb_ref[...]).astype(BF16)
        dl = (_rms(odl_ref[...]) * gdl_ref[...]).astype(BF16)
        x = x + jnp.dot(sb, wout_ref[:D_GROUP, :], preferred_element_type=F32)
        x = x + jnp.dot(dl, wout_ref[D_GROUP:, :], preferred_element_type=F32)
    gain_ref, wg_ref, wu_ref, wd_ref = refs[:4]
    h = (_rms(x) * gain_ref[...]).astype(BF16)
    acc = jnp.zeros(x.shape, F32)
    for c in range(0, D_FF, FF_CHUNK):
        g = jnp.dot(h, wg_ref[:, c:c + FF_CHUNK], preferred_element_type=F32)
        u = jnp.dot(h, wu_ref[:, c:c + FF_CHUNK], preferred_element_type=F32)
        a = (g * jax.nn.sigmoid(g) * u).astype(BF16)
        acc = acc + jnp.dot(a, wd_ref[c:c + FF_CHUNK, :], preferred_element_type=F32)
    y = x + HALF_STEP * acc
    if final_norm:
        y = _rms(y) * refs[4][...]
    o_ref[...] = y


def _ffn_block(x, gain, wg, wu, wd, mixer_out=None, final_gain=None):
    t = x.shape[0]
    row = lambda w: pl.BlockSpec((TM, w), lambda i: (i, 0))
    vec = lambda w: _resident((1, w))
    in_specs, args = [row(D_MODEL)], [x]
    if mixer_out is not None:
        o_sb, o_dl, g_sb, g_dl, w_out = mixer_out
        in_specs += [row(D_GROUP), row(D_GROUP), vec(D_GROUP), vec(D_GROUP), _resident((D_MODEL, D_MODEL))]
        args += [o_sb, o_dl, g_sb.reshape(1, D_GROUP), g_dl.reshape(1, D_GROUP), w_out]
    in_specs += [vec(D_MODEL), _resident((D_MODEL, D_FF)), _resident((D_MODEL, D_FF)), _resident((D_FF, D_MODEL))]
    args += [gain.reshape(1, D_MODEL), wg, wu, wd]
    if final_gain is not None:
        in_specs.append(vec(D_MODEL))
        args.append(final_gain.reshape(1, D_MODEL))
    return pl.pallas_call(
        functools.partial(_ffn_kernel, mixer_out=mixer_out is not None, final_norm=final_gain is not None),
        grid=(t // TM,),
        in_specs=in_specs,
        out_specs=row(D_MODEL),
        out_shape=jax.ShapeDtypeStruct((t, D_MODEL), F32),
        compiler_params=_params("parallel"),
        name="ffn_block",
    )(*args)


def _in_proj_kernel(x_ref, gain_ref, w_ref, cos_ref, sin_ref,
                    qsb_ref, ktsb_ref, vsb_ref, qdl_ref, kdl_ref, vdl_ref):
    h = (_rms(x_ref[...]) * gain_ref[...]).astype(BF16)

    def proj(j):
        return jnp.dot(h, w_ref[:, j * D_GROUP:(j + 1) * D_GROUP], preferred_element_type=F32)

    cos = cos_ref[...]
    sin = sin_ref[...]
    lane = lax.broadcasted_iota(jnp.int32, cos.shape, 1)
    low_half = (lane % HEAD_DIM) < HEAD_DIM // 2

    def rope(t):
        outs = []
        for j in range(D_GROUP // LANES):
            tj = t[:, j * LANES:(j + 1) * LANES]
            ahead = pltpu.roll(tj, LANES - HEAD_DIM // 2, 1)
            behind = pltpu.roll(tj, HEAD_DIM // 2, 1)
            outs.append(tj * cos + jnp.where(low_half, ahead, behind) * sin)
        return jnp.concatenate(outs, axis=1)

    qsb_ref[...] = (proj(0) * Q_SCALE).astype(BF16)
    kt = proj(1).T
    for hp in range(D_GROUP // LANES):
        for kb in range(TM // SB_KB):
            ktsb_ref[hp, kb] = kt[hp * LANES:(hp + 1) * LANES, kb * SB_KB:(kb + 1) * SB_KB].astype(BF16)
    vsb_ref[...] = proj(2).astype(BF16)
    qdl_ref[...] = (rope(proj(3)) * Q_SCALE).astype(BF16)
    kdl_ref[...] = rope(proj(4)).astype(BF16)
    vdl_ref[...] = proj(5).astype(BF16)


def _in_proj(x, gain, w_in, cos, sin, batch, seq):
    t = x.shape[0]
    n_seq_tiles = seq // TM
    row = lambda w: pl.BlockSpec((TM, w), lambda b, i: (b * n_seq_tiles + i, 0))
    grp = jax.ShapeDtypeStruct((t, D_GROUP), BF16)
    kt_shape = (batch, D_GROUP // LANES, seq // SB_KB, LANES, SB_KB)
    return pl.pallas_call(
        _in_proj_kernel,
        grid=(batch, n_seq_tiles),
        in_specs=[row(D_MODEL), _resident((1, D_MODEL)), _resident((D_MODEL, D_IN)),
                  pl.BlockSpec((TM, LANES), lambda b, i: (i, 0)),
                  pl.BlockSpec((TM, LANES), lambda b, i: (i, 0))],
        out_specs=[row(D_GROUP),
                   pl.BlockSpec((None, D_GROUP // LANES, TM // SB_KB, LANES, SB_KB),
                                lambda b, i: (b, 0, i, 0, 0)),
                   row(D_GROUP), row(D_GROUP), row(D_GROUP), row(D_GROUP)],
        out_shape=[grp, jax.ShapeDtypeStruct(kt_shape, BF16), grp, grp, grp, grp],
        compiler_params=_params("parallel", "parallel"),
        name="in_proj",
    )(x, gain.reshape(1, D_MODEL), w_in, cos, sin)


def _sb_kernel(q_ref, kt_ref, v_ref, csum_ref, o_ref, qs_ref, acc_ref, run_ref):
    i = pl.program_id(2)
    n_heads = LANES // HEAD_DIM
    lane = lax.broadcasted_iota(jnp.int32, (SB_TQ, LANES), 1)
    for hp in range(SB_PAIRS):
        q = q_ref[:, hp * LANES:(hp + 1) * LANES]
        for hh in range(n_heads):
            in_head = (lane >= hh * HEAD_DIM) & (lane < (hh + 1) * HEAD_DIM)
            qs_ref[hp, hh * SB_TQ:(hh + 1) * SB_TQ, :] = jnp.where(in_head, q, jnp.zeros_like(q))
    run_ref[...] = jnp.zeros_like(run_ref)
    acc_ref[...] = jnp.zeros_like(acc_ref)
    csum_w = csum_ref[...]

    def scores(hp, kb):
        return jnp.dot(qs_ref[hp], kt_ref[hp, kb], preferred_element_type=F32)

    def weights(hp, w, mask):
        p = jnp.maximum(w, 0.0) + jnp.log(1.0 + jnp.exp2(-jnp.abs(w))) * INV_LN2
        if mask is not None:
            p = jnp.where(mask, p, 0.0)
        neg_c = jnp.dot(p.astype(BF16), csum_w, preferred_element_type=F32)
        run = run_ref[hp]
        a = jnp.exp2((w - p) + neg_c + jnp.concatenate([run] * (SB_KB // LANES), axis=1))
        if mask is not None:
            a = jnp.where(mask, a, 0.0)
        run_ref[hp] = run + jnp.broadcast_to(neg_c[:, 0:1] - p[:, 0:1], run.shape)
        return a.astype(BF16)

    def accumulate(hp, a, kb):
        v = v_ref[pl.ds(pl.multiple_of(kb * SB_KB, SB_KB), SB_KB), hp * LANES:(hp + 1) * LANES]
        acc_ref[hp] += jnp.dot(a, v, preferred_element_type=F32)

    def block(kb, mask):
        for hp in range(SB_PAIRS):
            accumulate(hp, weights(hp, scores(hp, kb), mask), kb)

    row = lax.broadcasted_iota(jnp.int32, (n_heads * SB_TQ, SB_KB), 0) % SB_TQ
    col = lax.broadcasted_iota(jnp.int32, (n_heads * SB_TQ, SB_KB), 1)
    block(i, col < row)

    run_ref[...] = jnp.where(i > 0, run_ref[...], 2 * SB_DEAD_LOG2)
    block(jnp.maximum(i - 1, 0), None)

    def alive():
        return (jnp.max(run_ref[...]) > SB_DEAD_LOG2).astype(jnp.int32)

    def older(carry):
        kb, _ = carry
        block(kb, None)
        return kb - 1, alive()

    lax.while_loop(lambda c: (c[0] >= 0) & (c[1] > 0), older, (i - 2, alive()))
    for hp in range(SB_PAIRS):
        o_ref[:, hp * LANES:(hp + 1) * LANES] = jnp.where(lane < HEAD_DIM, acc_ref[hp, :SB_TQ, :], acc_ref[hp, SB_TQ:, :])


def _csum_weights():
    j = jnp.arange(SB_KB)[:, None]
    s = jnp.arange(SB_KB)[None, :]
    return jnp.where(j > s, -1.0, 0.0).astype(BF16)


def _sb_attention(q, kt, v, batch, seq):
    t = q.shape[0]
    n_q = seq // SB_TQ
    n_groups = D_GROUP // (SB_PAIRS * LANES)
    m = (LANES // HEAD_DIM) * SB_TQ
    return pl.pallas_call(
        _sb_kernel,
        grid=(batch, n_groups, n_q),
        in_specs=[pl.BlockSpec((SB_TQ, SB_PAIRS * LANES), lambda b, g, i: (b * n_q + i, g)),
                  pl.BlockSpec((None, SB_PAIRS, seq // SB_KB, LANES, SB_KB), lambda b, g, i: (b, g, 0, 0, 0)),
                  pl.BlockSpec((seq, SB_PAIRS * LANES), lambda b, g, i: (b, g)),
                  _resident((SB_KB, SB_KB))],
        out_specs=pl.BlockSpec((SB_TQ, SB_PAIRS * LANES), lambda b, g, i: (b * n_q + i, g)),
        out_shape=jax.ShapeDtypeStruct((t, D_GROUP), F32),
        scratch_shapes=[pltpu.VMEM((SB_PAIRS, m, LANES), BF16),
                        pltpu.VMEM((SB_PAIRS, m, LANES), F32),
                        pltpu.VMEM((SB_PAIRS, m, LANES), F32)],
        compiler_params=_params("parallel", "parallel", "arbitrary"),
        name="sb_attention",
    )(q, kt, v, _csum_weights())


def _dilated_kernel(q_ref, kp_ref, kc_ref, vp_ref, vc_ref, o_ref,
                    qn_ref, kn_ref, vn_ref, on4_ref, ln4_ref, on16_ref, ln16_ref):
    st = pl.program_id(2)
    n_heads = LANES // HEAD_DIM
    lane = lax.broadcasted_iota(jnp.int32, (DL_TQ, LANES), 1)
    row = lax.broadcasted_iota(jnp.int32, (DL_TQ, 2 * DL_TQ), 0)
    col = lax.broadcasted_iota(jnp.int32, (DL_TQ, 2 * DL_TQ), 1)
    band = (col >= row) & (col <= row + DL_TQ)
    band_edge = band & (col >= jnp.where(st == 0, DL_TQ, 0))
    head_lanes = [(lane >= hh * HEAD_DIM) & (lane < (hh + 1) * HEAD_DIM) for hh in range(n_heads)]

    qn_ref[...] = q_ref[...].astype(F32)
    kn_ref[:DL_SPAN, :] = kp_ref[...].astype(F32)
    kn_ref[DL_SPAN:, :] = kc_ref[...].astype(F32)
    vn_ref[:DL_SPAN, :] = vp_ref[...].astype(F32)
    vn_ref[DL_SPAN:, :] = vc_ref[...].astype(F32)

    def attend(q, k, v, valid):
        qs = jnp.concatenate([jnp.where(head_lanes[hh], q, jnp.zeros_like(q)) for hh in range(n_heads)], axis=0)
        z = lax.dot_general(qs, k, (((1,), (1,)), ((), ())), preferred_element_type=F32)
        z = jnp.where(jnp.concatenate([valid] * n_heads, axis=0), z, -jnp.inf)
        m = jnp.max(z, axis=-1, keepdims=True)
        p = jnp.exp2(z - m)
        denom = jnp.sum(p, axis=-1, keepdims=True)
        o = jnp.dot(p.astype(BF16), v, preferred_element_type=F32) / denom
        lse = jnp.broadcast_to(m + jnp.log(denom) * INV_LN2, o.shape)
        return (jnp.where(lane < HEAD_DIM, o[:DL_TQ], o[DL_TQ:]), jnp.where(lane < HEAD_DIM, lse[:DL_TQ], lse[DL_TQ:]))

    def strided_pattern(d, on_ref, ln_ref):
        for c in range(d):
            for j in range(DL_SPAN // (d * DL_TQ)):
                q0 = c + d * DL_TQ * j
                k0 = DL_SPAN + q0 - d * DL_TQ
                q = qn_ref[pl.ds(q0, DL_TQ, stride=d), :].astype(BF16)
                k = kn_ref[pl.ds(k0, 2 * DL_TQ, stride=d), :].astype(BF16)
                v = vn_ref[pl.ds(k0, 2 * DL_TQ, stride=d), :].astype(BF16)
                o, lse = attend(q, k, v, band_edge if j == 0 else band)
                on_ref[pl.ds(q0, DL_TQ, stride=d), :] = o
                ln_ref[pl.ds(q0, DL_TQ, stride=d), :] = lse

    strided_pattern(16, on16_ref, ln16_ref)
    strided_pattern(4, on4_ref, ln4_ref)

    def mix(o1, l1, r0):
        o4, l4 = on4_ref[pl.ds(r0, DL_TQ), :], ln4_ref[pl.ds(r0, DL_TQ), :]
        o16, l16 = on16_ref[pl.ds(r0, DL_TQ), :], ln16_ref[pl.ds(r0, DL_TQ), :]
        top = jnp.maximum(jnp.maximum(l1, l4), l16)
        e1, e4, e16 = jnp.exp2(l1 - top), jnp.exp2(l4 - top), jnp.exp2(l16 - top)
        return (e1 * o1 + e4 * o4 + e16 * o16) / (e1 + e4 + e16)

    k_edge = jnp.concatenate([kp_ref[DL_SPAN - DL_TQ:, :], kc_ref[:DL_TQ, :]], axis=0)
    v_edge = jnp.concatenate([vp_ref[DL_SPAN - DL_TQ:, :], vc_ref[:DL_TQ, :]], axis=0)
    o1, l1 = attend(q_ref[:DL_TQ, :], k_edge, v_edge, band_edge)
    o_ref[:DL_TQ, :] = mix(o1, l1, 0)

    for j in range(1, DL_SPAN // DL_TQ):
        q0, k0 = j * DL_TQ, (j - 1) * DL_TQ
        o1, l1 = attend(q_ref[q0:q0 + DL_TQ, :], kc_ref[k0:k0 + 2 * DL_TQ, :], vc_ref[k0:k0 + 2 * DL_TQ, :], band)
        o_ref[q0:q0 + DL_TQ, :] = mix(o1, l1, q0)


def _dilated_attention(q, k, v, batch, seq):
    t = q.shape[0]
    assert seq % DL_SPAN == 0
    n_span = seq // DL_SPAN
    cur = pl.BlockSpec((DL_SPAN, LANES), lambda b, hp, st: (b * n_span + st, hp))
    prev = pl.BlockSpec((DL_SPAN, LANES), lambda b, hp, st: (b * n_span + jnp.maximum(st - 1, 0), hp))
    nat = pltpu.VMEM((DL_SPAN, LANES), F32)
    return pl.pallas_call(
        _dilated_kernel,
        grid=(batch, D_GROUP // LANES, n_span),
        in_specs=[cur, prev, cur, prev, cur],
        out_specs=cur,
        out_shape=jax.ShapeDtypeStruct((t, D_GROUP), F32),
        scratch_shapes=[nat, pltpu.VMEM((2 * DL_SPAN, LANES), F32), pltpu.VMEM((2 * DL_SPAN, LANES), F32),
                        nat, nat, nat, nat],
        compiler_params=_params("parallel", "parallel", "arbitrary"),
        name="dilated_mixture",
    )(q, k, k, v, v)


def _rope_tables(seq):
    half = HEAD_DIM // 2
    inv_freq = ROPE_THETA ** (-jnp.arange(half, dtype=F32) / half)
    ang = jnp.arange(seq).astype(F32)[:, None] * inv_freq[None, :]
    cos, sin = jnp.cos(ang), jnp.sin(ang)
    reps = LANES // HEAD_DIM
    cos_t = jnp.tile(jnp.concatenate([cos, cos], axis=1), (1, reps))
    sin_t = jnp.tile(jnp.concatenate([-sin, sin], axis=1), (1, reps))
    return cos_t, sin_t


def kernel(x, ffn1_norm, ffn1_w_gate, ffn1_w_up, ffn1_w_down, mix_norm, w_in, sb_out_norm, dil_out_norm, w_out, ffn2_norm, ffn2_w_gate, ffn2_w_up, ffn2_w_down, final_norm):
    batch, seq, _ = x.shape
    depth = ffn1_norm.shape[0]
    cos, sin = _rope_tables(seq)
    h = x.reshape(batch * seq, D_MODEL)
    for layer in range(depth):
        h = _ffn_block(h, ffn1_norm[layer], ffn1_w_gate[layer].astype(BF16),
                       ffn1_w_up[layer].astype(BF16), ffn1_w_down[layer].astype(BF16))
        q_sb, kt_sb, v_sb, q_dl, k_dl, v_dl = _in_proj(
            h, mix_norm[layer], w_in[layer].astype(BF16), cos, sin, batch, seq)
        o_sb = _sb_attention(q_sb, kt_sb, v_sb, batch, seq)
        o_dl = _dilated_attention(q_dl, k_dl, v_dl, batch, seq)
        last = layer == depth - 1
        h = _ffn_block(h, ffn2_norm[layer], ffn2_w_gate[layer].astype(BF16),
                       ffn2_w_up[layer].astype(BF16), ffn2_w_down[layer].astype(BF16),
                       mixer_out=(o_sb, o_dl, sb_out_norm[layer], dil_out_norm[layer], w_out[layer].astype(BF16)),
                       final_gain=final_norm if last else None)
    return h.reshape(batch, seq, D_MODEL)
```

```python
import functools
import math

import jax
import jax.numpy as jnp
from jax import lax
from jax.experimental import pallas as pl
from jax.experimental.pallas import tpu as pltpu

D_MODEL = 1024
HEAD_DIM = 64
D_GROUP = 512
D_IN = 6 * D_GROUP
D_FF = 2816
DILATED_PATTERNS = ((128, 1), (512, 4), (2048, 16))
ROPE_THETA = 10000.0
RMS_EPS = 1e-6
HALF_STEP = 0.5

LANES = 128
LOG2E = math.log2(math.e)
INV_LN2 = 1.0 / math.log(2.0)
Q_SCALE = HEAD_DIM ** -0.5 * LOG2E

TM = 512
FF_CHUNK = 256
SB_TQ = 256
SB_PAIRS = 4
SB_KB = 256
SB_DEAD_LOG2 = -160.0
assert SB_TQ == SB_KB
DL_TQ = 128
DL_SPAN = DL_TQ * max(d for _, d in DILATED_PATTERNS)
assert [d for _, d in DILATED_PATTERNS] == [1, 4, 16] and all(w // d == DL_TQ for w, d in DILATED_PATTERNS)
CAST_STEPS = 16
VMEM_LIMIT = 48 * 1024 * 1024

F32 = jnp.float32
BF16 = jnp.bfloat16


def _rms(x):
    return x * lax.rsqrt(jnp.mean(x * x, axis=-1, keepdims=True) + RMS_EPS)


def _params(*sem):
    return pltpu.CompilerParams(dimension_semantics=sem, vmem_limit_bytes=VMEM_LIMIT)


def _resident(shape):
    return pl.BlockSpec(shape, lambda *_: (0,) * len(shape), pipeline_mode=pl.Buffered(1))


def _ffn_kernel(*refs, mixer_out, final_norm):
    refs = list(refs)
    o_ref = refs.pop()
    x = refs.pop(0)[...]
    if mixer_out:
        osb_ref, odl_ref, gsb_ref, gdl_ref, wout_ref = refs[:5]
        del refs[:5]
        sb = (_rms(osb_ref[...]) * gsb_ref[...]).astype(BF16)
        dl = (_rms(odl_ref[...]) * gdl_ref[...]).astype(BF16)
        x = x + jnp.dot(sb, wout_ref[:D_GROUP, :], preferred_element_type=F32)
        x = x + jnp.dot(dl, wout_ref[D_GROUP:, :], preferred_element_type=F32)
    gain_ref, wg_ref, wu_ref, wd_ref = refs[:4]
    h = (_rms(x) * gain_ref[...]).astype(BF16)
    acc = jnp.zeros(x.shape, F32)
    for c in range(0, D_FF, FF_CHUNK):
        g = jnp.dot(h, wg_ref[:, c:c + FF_CHUNK], preferred_element_type=F32)
        u = jnp.dot(h, wu_ref[:, c:c + FF_CHUNK], preferred_element_type=F32)
        a = (g * jax.nn.sigmoid(g) * u).astype(BF16)
        acc = acc + jnp.dot(a, wd_ref[c:c + FF_CHUNK, :], preferred_element_type=F32)
    y = x + HALF_STEP * acc
    if final_norm:
        y = _rms(y) * refs[4][...]
    o_ref[...] = y


def _ffn_block(x, gain, wg, wu, wd, mixer_out=None, final_gain=None):
    t = x.shape[0]
    row = lambda w: pl.BlockSpec((TM, w), lambda i: (i, 0))
    vec = lambda w: _resident((1, w))
    in_specs, args = [row(D_MODEL)], [x]
    if mixer_out is not None:
        o_sb, o_dl, g_sb, g_dl, w_out = mixer_out
        in_specs += [row(D_GROUP), row(D_GROUP), vec(D_GROUP), vec(D_GROUP), _resident((D_MODEL, D_MODEL))]
        args += [o_sb, o_dl, g_sb.reshape(1, D_GROUP), g_dl.reshape(1, D_GROUP), w_out]
    in_specs += [vec(D_MODEL), _resident((D_MODEL, D_FF)), _resident((D_MODEL, D_FF)), _resident((D_FF, D_MODEL))]
    args += [gain.reshape(1, D_MODEL), wg, wu, wd]
    if final_gain is not None:
        in_specs.append(vec(D_MODEL))
        args.append(final_gain.reshape(1, D_MODEL))
    return pl.pallas_call(
        functools.partial(_ffn_kernel, mixer_out=mixer_out is not None, final_norm=final_gain is not None),
        grid=(t // TM,),
        in_specs=in_specs,
        out_specs=row(D_MODEL),
        out_shape=jax.ShapeDtypeStruct((t, D_MODEL), F32),
        compiler_params=_params("parallel"),
        name="ffn_block",
    )(*args)


def _in_proj_kernel(x_ref, gain_ref, w_ref, cos_ref, sin_ref,
                    qsb_ref, ktsb_ref, vsb_ref, qdl_ref, kdl_ref, vdl_ref):
    h = (_rms(x_ref[...]) * gain_ref[...]).astype(BF16)

    def proj(j):
        return jnp.dot(h, w_ref[:, j * D_GROUP:(j + 1) * D_GROUP], preferred_element_type=F32)

    cos = cos_ref[...]
    sin = sin_ref[...]
    lane = lax.broadcasted_iota(jnp.int32, cos.shape, 1)
    low_half = (lane % HEAD_DIM) < HEAD_DIM // 2

    def rope(t):
        outs = []
        for j in range(D_GROUP // LANES):
            tj = t[:, j * LANES:(j + 1) * LANES]
            ahead = pltpu.roll(tj, LANES - HEAD_DIM // 2, 1)
            behind = pltpu.roll(tj, HEAD_DIM // 2, 1)
            outs.append(tj * cos + jnp.where(low_half, ahead, behind) * sin)
        return jnp.concatenate(outs, axis=1)

    qsb_ref[...] = (proj(0) * Q_SCALE).astype(BF16)
    kt = proj(1).T
    for hp in range(D_GROUP // LANES):
        for kb in range(TM // SB_KB):
            ktsb_ref[hp, kb] = kt[hp * LANES:(hp + 1) * LANES, kb * SB_KB:(kb + 1) * SB_KB].astype(BF16)
    vsb_ref[...] = proj(2).astype(BF16)
    qdl_ref[...] = (rope(proj(3)) * Q_SCALE).astype(BF16)
    kdl_ref[...] = rope(proj(4)).astype(BF16)
    vdl_ref[...] = proj(5).astype(BF16)


def _in_proj(x, gain, w_in, cos, sin, batch, seq):
    t = x.shape[0]
    n_seq_tiles = seq // TM
    row = lambda w: pl.BlockSpec((TM, w), lambda b, i: (b * n_seq_tiles + i, 0))
    grp = jax.ShapeDtypeStruct((t, D_GROUP), BF16)
    kt_shape = (batch, D_GROUP // LANES, seq // SB_KB, LANES, SB_KB)
    return pl.pallas_call(
        _in_proj_kernel,
        grid=(batch, n_seq_tiles),
        in_specs=[row(D_MODEL), _resident((1, D_MODEL)), _resident((D_MODEL, D_IN)),
                  pl.BlockSpec((TM, LANES), lambda b, i: (i, 0)),
                  pl.BlockSpec((TM, LANES), lambda b, i: (i, 0))],
        out_specs=[row(D_GROUP),
                   pl.BlockSpec((None, D_GROUP // LANES, TM // SB_KB, LANES, SB_KB),
                                lambda b, i: (b, 0, i, 0, 0)),
                   row(D_GROUP), row(D_GROUP), row(D_GROUP), row(D_GROUP)],
        out_shape=[grp, jax.ShapeDtypeStruct(kt_shape, BF16), grp, grp, grp, grp],
        compiler_params=_params("parallel", "parallel"),
        name="in_proj",
    )(x, gain.reshape(1, D_MODEL), w_in, cos, sin)


def _sb_kernel(q_ref, kt_ref, v_ref, csum_ref, o_ref, qs_ref, acc_ref, run_ref):
    i = pl.program_id(2)
    n_heads = LANES // HEAD_DIM
    lane = lax.broadcasted_iota(jnp.int32, (SB_TQ, LANES), 1)
    for hp in range(SB_PAIRS):
        q = q_ref[:, hp * LANES:(hp + 1) * LANES]
        for hh in range(n_heads):
            in_head = (lane >= hh * HEAD_DIM) & (lane < (hh + 1) * HEAD_DIM)
            qs_ref[hp, hh * SB_TQ:(hh + 1) * SB_TQ, :] = jnp.where(in_head, q, jnp.zeros_like(q))
    run_ref[...] = jnp.zeros_like(run_ref)
    acc_ref[...] = jnp.zeros_like(acc_ref)
    csum_w = csum_ref[...]

    def scores(hp, kb):
        return jnp.dot(qs_ref[hp], kt_ref[hp, kb], preferred_element_type=F32)

    def weights(hp, w, mask):
        p = jnp.maximum(w, 0.0) + jnp.log(1.0 + jnp.exp2(-jnp.abs(w))) * INV_LN2
        if mask is not None:
            p = jnp.where(mask, p, 0.0)
        neg_c = jnp.dot(p.astype(BF16), csum_w, preferred_element_type=F32)
        run = run_ref[hp]
        a = jnp.exp2((w - p) + neg_c + jnp.concatenate([run] * (SB_KB // LANES), axis=1))
        if mask is not None:
            a = jnp.where(mask, a, 0.0)
        run_ref[hp] = run + jnp.broadcast_to(neg_c[:, 0:1] - p[:, 0:1], run.shape)
        return a.astype(BF16)

    def accumulate(hp, a, kb):
        v = v_ref[pl.ds(pl.multiple_of(kb * SB_KB, SB_KB), SB_KB), hp * LANES:(hp + 1) * LANES]
        acc_ref[hp] += jnp.dot(a, v, preferred_element_type=F32)

    def block(kb, mask):
        for hp in range(SB_PAIRS):
            accumulate(hp, weights(hp, scores(hp, kb), mask), kb)

    row = lax.broadcasted_iota(jnp.int32, (n_heads * SB_TQ, SB_KB), 0) % SB_TQ
    col = lax.broadcasted_iota(jnp.int32, (n_heads * SB_TQ, SB_KB), 1)
    block(i, col < row)

    run_ref[...] = jnp.where(i > 0, run_ref[...], 2 * SB_DEAD_LOG2)
    block(jnp.maximum(i - 1, 0), None)

    def alive():
        return (jnp.max(run_ref[...]) > SB_DEAD_LOG2).astype(jnp.int32)

    def older(carry):
        kb, _ = carry
        block(kb, None)
        return kb - 1, alive()

    lax.while_loop(lambda c: (c[0] >= 0) & (c[1] > 0), older, (i - 2, alive()))
    for hp in range(SB_PAIRS):
        o_ref[:, hp * LANES:(hp + 1) * LANES] = jnp.where(lane < HEAD_DIM, acc_ref[hp, :SB_TQ, :], acc_ref[hp, SB_TQ:, :])


def _csum_weights():
    j = jnp.arange(SB_KB)[:, None]
    s = jnp.arange(SB_KB)[None, :]
    return jnp.where(j > s, -1.0, 0.0).astype(BF16)


def _sb_attention(q, kt, v, batch, seq):
    t = q.shape[0]
    n_q = seq // SB_TQ
    n_groups = D_GROUP // (SB_PAIRS * LANES)
    m = (LANES // HEAD_DIM) * SB_TQ
    return pl.pallas_call(
        _sb_kernel,
        grid=(batch, n_groups, n_q),
        in_specs=[pl.BlockSpec((SB_TQ, SB_PAIRS * LANES), lambda b, g, i: (b * n_q + i, g)),
                  pl.BlockSpec((None, SB_PAIRS, seq // SB_KB, LANES, SB_KB), lambda b, g, i: (b, g, 0, 0, 0)),
                  pl.BlockSpec((seq, SB_PAIRS * LANES), lambda b, g, i: (b, g)),
                  _resident((SB_KB, SB_KB))],
        out_specs=pl.BlockSpec((SB_TQ, SB_PAIRS * LANES), lambda b, g, i: (b * n_q + i, g)),
        out_shape=jax.ShapeDtypeStruct((t, D_GROUP), F32),
        scratch_shapes=[pltpu.VMEM((SB_PAIRS, m, LANES), BF16),
                        pltpu.VMEM((SB_PAIRS, m, LANES), F32),
                        pltpu.VMEM((SB_PAIRS, m, LANES), F32)],
        compiler_params=_params("parallel", "parallel", "arbitrary"),
        name="sb_attention",
    )(q, kt, v, _csum_weights())


def _dilated_kernel(q_ref, kp_ref, kc_ref, vp_ref, vc_ref, o_ref,
                    qn_ref, kn_ref, vn_ref, q4_ref, k4_ref, v4_ref, on4_ref, ln4_ref, on16_ref, ln16_ref):
    st = pl.program_id(2)
    n_heads = LANES // HEAD_DIM
    lane = lax.broadcasted_iota(jnp.int32, (DL_TQ, LANES), 1)
    row = lax.broadcasted_iota(jnp.int32, (DL_TQ, 2 * DL_TQ), 0)
    col = lax.broadcasted_iota(jnp.int32, (DL_TQ, 2 * DL_TQ), 1)
    band = (col >= row) & (col <= row + DL_TQ)
    band_edge = band & (col >= jnp.where(st == 0, DL_TQ, 0))
    head_lanes = [(lane >= hh * HEAD_DIM) & (lane < (hh + 1) * HEAD_DIM) for hh in range(n_heads)]

    qn_ref[...] = q_ref[...].astype(F32)
    kn_ref[:DL_SPAN, :] = kp_ref[...].astype(F32)
    kn_ref[DL_SPAN:, :] = kc_ref[...].astype(F32)
    vn_ref[:DL_SPAN, :] = vp_ref[...].astype(F32)
    vn_ref[DL_SPAN:, :] = vc_ref[...].astype(F32)

    def attend(q, k, v, valid):
        qs = jnp.concatenate([jnp.where(head_lanes[hh], q, jnp.zeros_like(q)) for hh in range(n_heads)], axis=0)
        z = lax.dot_general(qs, k, (((1,), (1,)), ((), ())), preferred_element_type=F32)
        z = jnp.where(jnp.concatenate([valid] * n_heads, axis=0), z, -jnp.inf)
        m = jnp.max(z, axis=-1, keepdims=True)
        p = jnp.exp2(z - m)
        denom = jnp.sum(p, axis=-1, keepdims=True)
        o = jnp.dot(p.astype(BF16), v, preferred_element_type=F32) / denom
        lse = jnp.broadcast_to(m + jnp.log(denom) * INV_LN2, o.shape)
        return (jnp.where(lane < HEAD_DIM, o[:DL_TQ], o[DL_TQ:]), jnp.where(lane < HEAD_DIM, lse[:DL_TQ], lse[DL_TQ:]))

    for src, dst in ((qn_ref, q4_ref), (kn_ref, k4_ref), (vn_ref, v4_ref)):
        per_class = src.shape[0] // 4
        for c in range(4):
            dst[c * per_class:(c + 1) * per_class, :] = src[pl.ds(c, per_class, stride=4), :]

    def strided_pattern(d, on_ref, ln_ref):
        sub = d // 4
        q_per_class, k_per_class = DL_SPAN // 4, 2 * DL_SPAN // 4
        for c in range(d):
            c4, r = c % 4, c // 4
            for j in range(DL_SPAN // (d * DL_TQ)):
                q0 = c + d * DL_TQ * j
                qi = r + sub * DL_TQ * j
                ki = k_per_class // 2 + qi - sub * DL_TQ
                q = q4_ref[pl.ds(c4 * q_per_class + qi, DL_TQ, stride=sub), :].astype(BF16)
                k = k4_ref[pl.ds(c4 * k_per_class + ki, 2 * DL_TQ, stride=sub), :].astype(BF16)
                v = v4_ref[pl.ds(c4 * k_per_class + ki, 2 * DL_TQ, stride=sub), :].astype(BF16)
                o, lse = attend(q, k, v, band_edge if j == 0 else band)
                on_ref[pl.ds(q0, DL_TQ, stride=d), :] = o
                ln_ref[pl.ds(q0, DL_TQ, stride=d), :] = lse

    strided_pattern(16, on16_ref, ln16_ref)
    strided_pattern(4, on4_ref, ln4_ref)

    def mix(o1, l1, r0):
        o4, l4 = on4_ref[pl.ds(r0, DL_TQ), :], ln4_ref[pl.ds(r0, DL_TQ), :]
        o16, l16 = on16_ref[pl.ds(r0, DL_TQ), :], ln16_ref[pl.ds(r0, DL_TQ), :]
        top = jnp.maximum(jnp.maximum(l1, l4), l16)
        e1, e4, e16 = jnp.exp2(l1 - top), jnp.exp2(l4 - top), jnp.exp2(l16 - top)
        return (e1 * o1 + e4 * o4 + e16 * o16) / (e1 + e4 + e16)

    k_edge = jnp.concatenate([kp_ref[DL_SPAN - DL_TQ:, :], kc_ref[:DL_TQ, :]], axis=0)
    v_edge = jnp.concatenate([vp_ref[DL_SPAN - DL_TQ:, :], vc_ref[:DL_TQ, :]], axis=0)
    o1, l1 = attend(q_ref[:DL_TQ, :], k_edge, v_edge, band_edge)
    o_ref[:DL_TQ, :] = mix(o1, l1, 0)

    for j in range(1, DL_SPAN // DL_TQ):
        q0, k0 = j * DL_TQ, (j - 1) * DL_TQ
        o1, l1 = attend(q_ref[q0:q0 + DL_TQ, :], kc_ref[k0:k0 + 2 * DL_TQ, :], vc_ref[k0:k0 + 2 * DL_TQ, :], band)
        o_ref[q0:q0 + DL_TQ, :] = mix(o1, l1, q0)


def _dilated_attention(q, k, v, batch, seq):
    t = q.shape[0]
    assert seq % DL_SPAN == 0
    n_span = seq // DL_SPAN
    cur = pl.BlockSpec((DL_SPAN, LANES), lambda b, hp, st: (b * n_span + st, hp))
    prev = pl.BlockSpec((DL_SPAN, LANES), lambda b, hp, st: (b * n_span + jnp.maximum(st - 1, 0), hp))
    nat = pltpu.VMEM((DL_SPAN, LANES), F32)
    both = pltpu.VMEM((2 * DL_SPAN, LANES), F32)
    return pl.pallas_call(
        _dilated_kernel,
        grid=(batch, D_GROUP // LANES, n_span),
        in_specs=[cur, prev, cur, prev, cur],
        out_specs=cur,
        out_shape=jax.ShapeDtypeStruct((t, D_GROUP), F32),
        scratch_shapes=[nat, both, both, nat, both, both, nat, nat, nat, nat],
        compiler_params=_params("parallel", "parallel", "arbitrary"),
        name="dilated_mixture",
    )(q, k, k, v, v)


def _cast_kernel(*refs):
    n = len(refs) // 2
    for src, dst in zip(refs[:n], refs[n:]):
        dst[...] = src[...].astype(dst.dtype)


def _layer_weights_bf16(layer, *stacked):
    in_specs, out_specs, out_shape = [], [], []
    for w in stacked:
        _, rows, cols = w.shape
        assert rows % (CAST_STEPS * 16) == 0
        in_specs.append(pl.BlockSpec((None, rows // CAST_STEPS, cols), lambda i: (layer, i, 0)))
        out_specs.append(pl.BlockSpec((rows // CAST_STEPS, cols), lambda i: (i, 0)))
        out_shape.append(jax.ShapeDtypeStruct((rows, cols), BF16))
    return pl.pallas_call(
        _cast_kernel,
        grid=(CAST_STEPS,),
        in_specs=in_specs,
        out_specs=out_specs,
        out_shape=out_shape,
        compiler_params=_params("parallel"),
        name="weights_bf16",
    )(*stacked)


def _rope_tables(seq):
    half = HEAD_DIM // 2
    inv_freq = ROPE_THETA ** (-jnp.arange(half, dtype=F32) / half)
    ang = jnp.arange(seq).astype(F32)[:, None] * inv_freq[None, :]
    cos, sin = jnp.cos(ang), jnp.sin(ang)
    reps = LANES // HEAD_DIM
    cos_t = jnp.tile(jnp.concatenate([cos, cos], axis=1), (1, reps))
    sin_t = jnp.tile(jnp.concatenate([-sin, sin], axis=1), (1, reps))
    return cos_t, sin_t


def kernel(x, ffn1_norm, ffn1_w_gate, ffn1_w_up, ffn1_w_down, mix_norm, w_in, sb_out_norm, dil_out_norm, w_out, ffn2_norm, ffn2_w_gate, ffn2_w_up, ffn2_w_down, final_norm):
    batch, seq, _ = x.shape
    depth = ffn1_norm.shape[0]
    cos, sin = _rope_tables(seq)
    h = x.reshape(batch * seq, D_MODEL)
    for layer in range(depth):
        g1, u1, d1, w_qkv, w_o, g2, u2, d2 = _layer_weights_bf16(
            layer, ffn1_w_gate, ffn1_w_up, ffn1_w_down, w_in, w_out, ffn2_w_gate, ffn2_w_up, ffn2_w_down)
        h = _ffn_block(h, ffn1_norm[layer], g1, u1, d1)
        q_sb, kt_sb, v_sb, q_dl, k_dl, v_dl = _in_proj(h, mix_norm[layer], w_qkv, cos, sin, batch, seq)
        o_sb = _sb_attention(q_sb, kt_sb, v_sb, batch, seq)
        o_dl = _dilated_attention(q_dl, k_dl, v_dl, batch, seq)
        h = _ffn_block(h, ffn2_norm[layer], g2, u2, d2,
                       mixer_out=(o_sb, o_dl, sb_out_norm[layer], dil_out_norm[layer], w_o),
                       final_gain=final_norm if layer == depth - 1 else None)
    return h.reshape(batch, seq, D_MODEL)
```

```python
import functools
import math

import jax
import jax.numpy as jnp
import numpy as np
from jax import lax
from jax.experimental import pallas as pl
from jax.experimental.pallas import tpu as pltpu

D_MODEL = 1024
HEAD_DIM = 64
D_GROUP = 512
D_IN = 6 * D_GROUP
D_FF = 2816
DILATED_PATTERNS = ((128, 1), (512, 4), (2048, 16))
ROPE_THETA = 10000.0
RMS_EPS = 1e-6
HALF_STEP = 0.5

LANES = 128
LOG2E = math.log2(math.e)
INV_LN2 = 1.0 / math.log(2.0)
Q_SCALE = HEAD_DIM ** -0.5 * LOG2E

TM = 512
FF_CHUNK = 256
SB_TQ = 256
SB_PAIRS = 4
SB_KB = 256
SB_DEAD_LOG2 = -160.0
assert SB_TQ == SB_KB
DL_TQ = 128
DL_SPAN = DL_TQ * max(d for _, d in DILATED_PATTERNS)
assert [d for _, d in DILATED_PATTERNS] == [1, 4, 16] and all(w // d == DL_TQ for w, d in DILATED_PATTERNS)
CAST_STEPS = 8
VMEM_LIMIT = 48 * 1024 * 1024

F32 = jnp.float32
BF16 = jnp.bfloat16


def _rms(x):
    return x * lax.rsqrt(jnp.mean(x * x, axis=-1, keepdims=True) + RMS_EPS)


def _params(*sem):
    return pltpu.CompilerParams(dimension_semantics=sem, vmem_limit_bytes=VMEM_LIMIT)


def _resident(shape):
    return pl.BlockSpec(shape, lambda *_: (0,) * len(shape), pipeline_mode=pl.Buffered(1))


def _ffn_kernel(*refs, mixer_out, final_norm):
    refs = list(refs)
    o_ref = refs.pop()
    x = refs.pop(0)[...]
    if mixer_out:
        osb_ref, odl_ref, gsb_ref, gdl_ref, wout_ref = refs[:5]
        del refs[:5]
        sb = (_rms(osb_ref[...]) * gsb_ref[...]).astype(BF16)
        dl = (_rms(odl_ref[...]) * gdl_ref[...]).astype(BF16)
        x = x + jnp.dot(sb, wout_ref[:D_GROUP, :], preferred_element_type=F32)
        x = x + jnp.dot(dl, wout_ref[D_GROUP:, :], preferred_element_type=F32)
    gain_ref, wg_ref, wu_ref, wd_ref = refs[:4]
    h = (_rms(x) * gain_ref[...]).astype(BF16)
    acc = jnp.zeros(x.shape, F32)
    for c in range(0, D_FF, FF_CHUNK):
        g = jnp.dot(h, wg_ref[:, c:c + FF_CHUNK], preferred_element_type=F32)
        u = jnp.dot(h, wu_ref[:, c:c + FF_CHUNK], preferred_element_type=F32)
        a = (g * jax.nn.sigmoid(g) * u).astype(BF16)
        acc = acc + jnp.dot(a, wd_ref[c:c + FF_CHUNK, :], preferred_element_type=F32)
    y = x + HALF_STEP * acc
    if final_norm:
        y = _rms(y) * refs[4][...]
    o_ref[...] = y


def _ffn_block(x, gain, wg, wu, wd, mixer_out=None, final_gain=None):
    t = x.shape[0]
    row = lambda w: pl.BlockSpec((TM, w), lambda i: (i, 0))
    vec = lambda w: _resident((1, w))
    in_specs, args = [row(D_MODEL)], [x]
    if mixer_out is not None:
        o_sb, o_dl, g_sb, g_dl, w_out = mixer_out
        in_specs += [row(D_GROUP), row(D_GROUP), vec(D_GROUP), vec(D_GROUP), _resident((D_MODEL, D_MODEL))]
        args += [o_sb, o_dl, g_sb.reshape(1, D_GROUP), g_dl.reshape(1, D_GROUP), w_out]
    in_specs += [vec(D_MODEL), _resident((D_MODEL, D_FF)), _resident((D_MODEL, D_FF)), _resident((D_FF, D_MODEL))]
    args += [gain.reshape(1, D_MODEL), wg, wu, wd]
    if final_gain is not None:
        in_specs.append(vec(D_MODEL))
        args.append(final_gain.reshape(1, D_MODEL))
    return pl.pallas_call(
        functools.partial(_ffn_kernel, mixer_out=mixer_out is not None, final_norm=final_gain is not None),
        grid=(t // TM,),
        in_specs=in_specs,
        out_specs=row(D_MODEL),
        out_shape=jax.ShapeDtypeStruct((t, D_MODEL), F32),
        compiler_params=_params("parallel"),
        name="ffn_block",
    )(*args)


def _in_proj_kernel(x_ref, gain_ref, w_ref, cos_ref, sin_ref,
                    qsb_ref, ktsb_ref, vsb_ref, qdl_ref, kdl_ref, vdl_ref):
    h = (_rms(x_ref[...]) * gain_ref[...]).astype(BF16)

    def proj(j):
        return jnp.dot(h, w_ref[:, j * D_GROUP:(j + 1) * D_GROUP], preferred_element_type=F32)

    cos = cos_ref[...]
    sin = sin_ref[...]
    lane = lax.broadcasted_iota(jnp.int32, cos.shape, 1)
    low_half = (lane % HEAD_DIM) < HEAD_DIM // 2

    def rope(t):
        outs = []
        for j in range(D_GROUP // LANES):
            tj = t[:, j * LANES:(j + 1) * LANES]
            ahead = pltpu.roll(tj, LANES - HEAD_DIM // 2, 1)
            behind = pltpu.roll(tj, HEAD_DIM // 2, 1)
            outs.append(tj * cos + jnp.where(low_half, ahead, behind) * sin)
        return jnp.concatenate(outs, axis=1)

    qsb_ref[...] = (proj(0) * Q_SCALE).astype(BF16)
    kt = proj(1).T
    for hp in range(D_GROUP // LANES):
        for kb in range(TM // SB_KB):
            ktsb_ref[hp, kb] = kt[hp * LANES:(hp + 1) * LANES, kb * SB_KB:(kb + 1) * SB_KB].astype(BF16)
    vsb_ref[...] = proj(2).astype(BF16)
    qdl_ref[...] = (rope(proj(3)) * Q_SCALE).astype(BF16)
    kdl_ref[...] = rope(proj(4)).astype(BF16)
    vdl_ref[...] = proj(5).astype(BF16)


def _in_proj(x, gain, w_in, cos, sin, batch, seq):
    t = x.shape[0]
    n_seq_tiles = seq // TM
    row = lambda w: pl.BlockSpec((TM, w), lambda b, i: (b * n_seq_tiles + i, 0))
    grp = jax.ShapeDtypeStruct((t, D_GROUP), BF16)
    kt_shape = (batch, D_GROUP // LANES, seq // SB_KB, LANES, SB_KB)
    return pl.pallas_call(
        _in_proj_kernel,
        grid=(batch, n_seq_tiles),
        in_specs=[row(D_MODEL), _resident((1, D_MODEL)), _resident((D_MODEL, D_IN)),
                  pl.BlockSpec((TM, LANES), lambda b, i: (i, 0)),
                  pl.BlockSpec((TM, LANES), lambda b, i: (i, 0))],
        out_specs=[row(D_GROUP),
                   pl.BlockSpec((None, D_GROUP // LANES, TM // SB_KB, LANES, SB_KB),
                                lambda b, i: (b, 0, i, 0, 0)),
                   row(D_GROUP), row(D_GROUP), row(D_GROUP), row(D_GROUP)],
        out_shape=[grp, jax.ShapeDtypeStruct(kt_shape, BF16), grp, grp, grp, grp],
        compiler_params=_params("parallel", "parallel"),
        name="in_proj",
    )(x, gain.reshape(1, D_MODEL), w_in, cos, sin)


def _sb_kernel(q_ref, kt_ref, v_ref, csum_ref, o_ref, qs_ref, acc_ref, run_ref):
    i = pl.program_id(2)
    n_heads = LANES // HEAD_DIM
    lane = lax.broadcasted_iota(jnp.int32, (SB_TQ, LANES), 1)
    for hp in range(SB_PAIRS):
        q = q_ref[:, hp * LANES:(hp + 1) * LANES]
        for hh in range(n_heads):
            in_head = (lane >= hh * HEAD_DIM) & (lane < (hh + 1) * HEAD_DIM)
            qs_ref[hp, hh * SB_TQ:(hh + 1) * SB_TQ, :] = jnp.where(in_head, q, jnp.zeros_like(q))
    run_ref[...] = jnp.zeros_like(run_ref)
    acc_ref[...] = jnp.zeros_like(acc_ref)
    csum_w = csum_ref[...]

    def scores(hp, kb):
        return jnp.dot(qs_ref[hp], kt_ref[hp, kb], preferred_element_type=F32)

    def weights(hp, w, mask):
        p = jnp.maximum(w, 0.0) + jnp.log(1.0 + jnp.exp2(-jnp.abs(w))) * INV_LN2
        if mask is not None:
            p = jnp.where(mask, p, 0.0)
        neg_c = jnp.dot(p.astype(BF16), csum_w, preferred_element_type=F32)
        run = run_ref[hp]
        a = jnp.exp2((w - p) + neg_c + jnp.concatenate([run] * (SB_KB // LANES), axis=1))
        if mask is not None:
            a = jnp.where(mask, a, 0.0)
        run_ref[hp] = run + jnp.broadcast_to(neg_c[:, 0:1] - p[:, 0:1], run.shape)
        return a.astype(BF16)

    def accumulate(hp, a, kb):
        v = v_ref[pl.ds(pl.multiple_of(kb * SB_KB, SB_KB), SB_KB), hp * LANES:(hp + 1) * LANES]
        acc_ref[hp] += jnp.dot(a, v, preferred_element_type=F32)

    def block(kb, mask):
        for hp in range(SB_PAIRS):
            accumulate(hp, weights(hp, scores(hp, kb), mask), kb)

    row = lax.broadcasted_iota(jnp.int32, (n_heads * SB_TQ, SB_KB), 0) % SB_TQ
    col = lax.broadcasted_iota(jnp.int32, (n_heads * SB_TQ, SB_KB), 1)
    block(i, col < row)

    run_ref[...] = jnp.where(i > 0, run_ref[...], 2 * SB_DEAD_LOG2)
    block(jnp.maximum(i - 1, 0), None)

    def alive():
        return (jnp.max(run_ref[...]) > SB_DEAD_LOG2).astype(jnp.int32)

    def older(carry):
        kb, _ = carry
        block(kb, None)
        return kb - 1, alive()

    lax.while_loop(lambda c: (c[0] >= 0) & (c[1] > 0), older, (i - 2, alive()))
    for hp in range(SB_PAIRS):
        o_ref[:, hp * LANES:(hp + 1) * LANES] = jnp.where(lane < HEAD_DIM, acc_ref[hp, :SB_TQ, :], acc_ref[hp, SB_TQ:, :])


def _csum_weights():
    j = jnp.arange(SB_KB)[:, None]
    s = jnp.arange(SB_KB)[None, :]
    return jnp.where(j > s, -1.0, 0.0).astype(BF16)


def _sb_attention(q, kt, v, batch, seq):
    t = q.shape[0]
    n_q = seq // SB_TQ
    n_groups = D_GROUP // (SB_PAIRS * LANES)
    m = (LANES // HEAD_DIM) * SB_TQ
    return pl.pallas_call(
        _sb_kernel,
        grid=(batch, n_groups, n_q),
        in_specs=[pl.BlockSpec((SB_TQ, SB_PAIRS * LANES), lambda b, g, i: (b * n_q + i, g)),
                  pl.BlockSpec((None, SB_PAIRS, seq // SB_KB, LANES, SB_KB), lambda b, g, i: (b, g, 0, 0, 0)),
                  pl.BlockSpec((seq, SB_PAIRS * LANES), lambda b, g, i: (b, g)),
                  _resident((SB_KB, SB_KB))],
        out_specs=pl.BlockSpec((SB_TQ, SB_PAIRS * LANES), lambda b, g, i: (b * n_q + i, g)),
        out_shape=jax.ShapeDtypeStruct((t, D_GROUP), F32),
        scratch_shapes=[pltpu.VMEM((SB_PAIRS, m, LANES), BF16),
                        pltpu.VMEM((SB_PAIRS, m, LANES), F32),
                        pltpu.VMEM((SB_PAIRS, m, LANES), F32)],
        compiler_params=_params("parallel", "parallel", "arbitrary"),
        name="sb_attention",
    )(q, kt, v, _csum_weights())


def _dilated_kernel(q_ref, kp_ref, kc_ref, vp_ref, vc_ref, o_ref,
                    qn_ref, kn_ref, vn_ref, q4_ref, k4_ref, v4_ref,
                    on4_ref, mn4_ref, ln4_ref, on16_ref, mn16_ref, ln16_ref):
    st = pl.program_id(2)
    n_heads = LANES // HEAD_DIM
    lane = lax.broadcasted_iota(jnp.int32, (DL_TQ, LANES), 1)
    row = lax.broadcasted_iota(jnp.int32, (DL_TQ, 2 * DL_TQ), 0)
    col = lax.broadcasted_iota(jnp.int32, (DL_TQ, 2 * DL_TQ), 1)
    band = (col >= row) & (col <= row + DL_TQ)
    band_edge = band & (col >= jnp.where(st == 0, DL_TQ, 0))
    head_lanes = [(lane >= hh * HEAD_DIM) & (lane < (hh + 1) * HEAD_DIM) for hh in range(n_heads)]

    qn_ref[...] = q_ref[...].astype(F32)
    kn_ref[:DL_SPAN, :] = kp_ref[...].astype(F32)
    kn_ref[DL_SPAN:, :] = kc_ref[...].astype(F32)
    vn_ref[:DL_SPAN, :] = vp_ref[...].astype(F32)
    vn_ref[DL_SPAN:, :] = vc_ref[...].astype(F32)

    def attend(q, k, v, valid):
        qs = jnp.concatenate([jnp.where(head_lanes[hh], q, jnp.zeros_like(q)) for hh in range(n_heads)], axis=0)
        z = lax.dot_general(qs, k, (((1,), (1,)), ((), ())), preferred_element_type=F32)
        z = jnp.where(jnp.concatenate([valid] * n_heads, axis=0), z, -jnp.inf)
        m = jnp.max(z, axis=-1, keepdims=True)
        p = jnp.exp2(z - m)
        den = jnp.sum(p, axis=-1, keepdims=True)
        num = jnp.dot(p.astype(BF16), v, preferred_element_type=F32)
        m, den = jnp.broadcast_to(m, num.shape), jnp.broadcast_to(den, num.shape)
        return tuple(jnp.where(lane < HEAD_DIM, t[:DL_TQ], t[DL_TQ:]) for t in (num, m, den))

    for src, dst in ((qn_ref, q4_ref), (kn_ref, k4_ref), (vn_ref, v4_ref)):
        per_class = src.shape[0] // 4
        for c in range(4):
            dst[c * per_class:(c + 1) * per_class, :] = src[pl.ds(c, per_class, stride=4), :]

    def strided_pattern(d, on_ref, mn_ref, ln_ref):
        sub = d // 4
        q_per_class, k_per_class = DL_SPAN // 4, 2 * DL_SPAN // 4
        for c in range(d):
            c4, r = c % 4, c // 4
            for j in range(DL_SPAN // (d * DL_TQ)):
                q0 = c + d * DL_TQ * j
                qi = r + sub * DL_TQ * j
                ki = k_per_class // 2 + qi - sub * DL_TQ
                q = q4_ref[pl.ds(c4 * q_per_class + qi, DL_TQ, stride=sub), :].astype(BF16)
                k = k4_ref[pl.ds(c4 * k_per_class + ki, 2 * DL_TQ, stride=sub), :].astype(BF16)
                v = v4_ref[pl.ds(c4 * k_per_class + ki, 2 * DL_TQ, stride=sub), :].astype(BF16)
                num, m, den = attend(q, k, v, band_edge if j == 0 else band)
                on_ref[pl.ds(q0, DL_TQ, stride=d), :] = num
                mn_ref[pl.ds(q0, DL_TQ, stride=d), :] = m
                ln_ref[pl.ds(q0, DL_TQ, stride=d), :] = den

    strided_pattern(16, on16_ref, mn16_ref, ln16_ref)
    strided_pattern(4, on4_ref, mn4_ref, ln4_ref)

    def mix(part1, r0):
        rows = pl.ds(r0, DL_TQ)
        parts = (part1, (on4_ref[rows, :], mn4_ref[rows, :], ln4_ref[rows, :]),
                 (on16_ref[rows, :], mn16_ref[rows, :], ln16_ref[rows, :]))
        top = functools.reduce(jnp.maximum, [m for _, m, _ in parts])
        scale = [jnp.exp2(m - top) for _, m, _ in parts]
        num = sum(s * n for s, (n, _, _) in zip(scale, parts))
        den = sum(s * d for s, (_, _, d) in zip(scale, parts))
        return num / den

    k_edge = jnp.concatenate([kp_ref[DL_SPAN - DL_TQ:, :], kc_ref[:DL_TQ, :]], axis=0)
    v_edge = jnp.concatenate([vp_ref[DL_SPAN - DL_TQ:, :], vc_ref[:DL_TQ, :]], axis=0)
    o_ref[:DL_TQ, :] = mix(attend(q_ref[:DL_TQ, :], k_edge, v_edge, band_edge), 0)

    for j in range(1, DL_SPAN // DL_TQ):
        q0, k0 = j * DL_TQ, (j - 1) * DL_TQ
        part = attend(q_ref[q0:q0 + DL_TQ, :], kc_ref[k0:k0 + 2 * DL_TQ, :], vc_ref[k0:k0 + 2 * DL_TQ, :], band)
        o_ref[q0:q0 + DL_TQ, :] = mix(part, q0)


def _dilated_attention(q, k, v, batch, seq):
    t = q.shape[0]
    assert seq % DL_SPAN == 0
    n_span = seq // DL_SPAN
    cur = pl.BlockSpec((DL_SPAN, LANES), lambda b, hp, st: (b * n_span + st, hp))
    prev = pl.BlockSpec((DL_SPAN, LANES), lambda b, hp, st: (b * n_span + jnp.maximum(st - 1, 0), hp))
    nat = pltpu.VMEM((DL_SPAN, LANES), F32)
    both = pltpu.VMEM((2 * DL_SPAN, LANES), F32)
    return pl.pallas_call(
        _dilated_kernel,
        grid=(batch, D_GROUP // LANES, n_span),
        in_specs=[cur, prev, cur, prev, cur],
        out_specs=cur,
        out_shape=jax.ShapeDtypeStruct((t, D_GROUP), F32),
        scratch_shapes=[nat, both, both, nat, both, both, nat, nat, nat, nat, nat, nat],
        compiler_params=_params("parallel", "parallel", "arbitrary"),
        name="dilated_mixture",
    )(q, k, k, v, v)


def _cast_kernel(*refs):
    n = len(refs) // 2
    for src, dst in zip(refs[:n], refs[n:]):
        dst[...] = src[...].astype(dst.dtype)


def _layer_weights_bf16(layer, *stacked):
    in_specs, out_specs, out_shape = [], [], []
    for w in stacked:
        _, rows, cols = w.shape
        assert rows % (CAST_STEPS * 16) == 0
        in_specs.append(pl.BlockSpec((None, rows // CAST_STEPS, cols), lambda i: (layer, i, 0)))
        out_specs.append(pl.BlockSpec((rows // CAST_STEPS, cols), lambda i: (i, 0)))
        out_shape.append(jax.ShapeDtypeStruct((rows, cols), BF16))
    return pl.pallas_call(
        _cast_kernel,
        grid=(CAST_STEPS,),
        in_specs=in_specs,
        out_specs=out_specs,
        out_shape=out_shape,
        compiler_params=_params("parallel"),
        name="weights_bf16",
    )(*stacked)


def _rope_tables(seq):
    half = HEAD_DIM // 2
    inv_freq = ROPE_THETA ** (-np.arange(half, dtype=np.float64) / half)
    ang = np.arange(seq, dtype=np.float64)[:, None] * inv_freq[None, :]
    cos, sin = np.cos(ang), np.sin(ang)
    reps = LANES // HEAD_DIM
    cos_t = np.tile(np.concatenate([cos, cos], axis=1), (1, reps))
    sin_t = np.tile(np.concatenate([-sin, sin], axis=1), (1, reps))
    return jnp.asarray(cos_t, F32), jnp.asarray(sin_t, F32)


def kernel(x, ffn1_norm, ffn1_w_gate, ffn1_w_up, ffn1_w_down, mix_norm, w_in, sb_out_norm, dil_out_norm, w_out, ffn2_norm, ffn2_w_gate, ffn2_w_up, ffn2_w_down, final_norm):
    batch, seq, _ = x.shape
    depth = ffn1_norm.shape[0]
    cos, sin = _rope_tables(seq)
    h = x.reshape(batch * seq, D_MODEL)
    for layer in range(depth):
        g1, u1, d1, w_qkv, w_o, g2, u2, d2 = _layer_weights_bf16(
            layer, ffn1_w_gate, ffn1_w_up, ffn1_w_down, w_in, w_out, ffn2_w_gate, ffn2_w_up, ffn2_w_down)
        h = _ffn_block(h, ffn1_norm[layer], g1, u1, d1)
        q_sb, kt_sb, v_sb, q_dl, k_dl, v_dl = _in_proj(h, mix_norm[layer], w_qkv, cos, sin, batch, seq)
        o_sb = _sb_attention(q_sb, kt_sb, v_sb, batch, seq)
        o_dl = _dilated_attention(q_dl, k_dl, v_dl, batch, seq)
        h = _ffn_block(h, ffn2_norm[layer], g2, u2, d2,
                       mixer_out=(o_sb, o_dl, sb_out_norm[layer], dil_out_norm[layer], w_o),
                       final_gain=final_norm if layer == depth - 1 else None)
    return h.reshape(batch, seq, D_MODEL)
```

```python
import functools
import math

import jax
import jax.numpy as jnp
import numpy as np
from jax import lax
from jax.experimental import pallas as pl
from jax.experimental.pallas import tpu as pltpu

D_MODEL = 1024
HEAD_DIM = 64
D_GROUP = 512
D_IN = 6 * D_GROUP
D_FF = 2816
DILATED_PATTERNS = ((128, 1), (512, 4), (2048, 16))
ROPE_THETA = 10000.0
RMS_EPS = 1e-6
HALF_STEP = 0.5

LANES = 128
LOG2E = math.log2(math.e)
INV_LN2 = 1.0 / math.log(2.0)
Q_SCALE = HEAD_DIM ** -0.5 * LOG2E

TM = 512
PROJ_TM = 1024
FF_CHUNK = 256
SB_TQ = 256
SB_PAIRS = 4
SB_KB = 256
SB_DEAD_LOG2 = -160.0
assert SB_TQ == SB_KB
DL_TQ = 128
DL_SPAN = DL_TQ * max(d for _, d in DILATED_PATTERNS)
assert [d for _, d in DILATED_PATTERNS] == [1, 4, 16] and all(w // d == DL_TQ for w, d in DILATED_PATTERNS)
CAST_STEPS = 8
VMEM_LIMIT = 48 * 1024 * 1024

F32 = jnp.float32
BF16 = jnp.bfloat16


def _rms(x):
    return x * lax.rsqrt(jnp.mean(x * x, axis=-1, keepdims=True) + RMS_EPS)


def _params(*sem):
    return pltpu.CompilerParams(dimension_semantics=sem, vmem_limit_bytes=VMEM_LIMIT)


def _resident(shape):
    return pl.BlockSpec(shape, lambda *_: (0,) * len(shape), pipeline_mode=pl.Buffered(1))


def _ffn_kernel(*refs, mixer_out, final_norm, n_carried):
    refs = list(refs)
    carried_out = [refs.pop() for _ in range(n_carried)][::-1]
    o_ref = refs.pop()
    carried_in = [refs.pop() for _ in range(n_carried)][::-1]
    for src, dst in zip(carried_in, carried_out):
        dst[...] = src[...].astype(dst.dtype)
    x = refs.pop(0)[...]
    if mixer_out:
        osb_ref, odl_ref, gsb_ref, gdl_ref, wout_ref = refs[:5]
        del refs[:5]
        sb = (_rms(osb_ref[...]) * gsb_ref[...]).astype(BF16)
        dl = (_rms(odl_ref[...]) * gdl_ref[...]).astype(BF16)
        x = x + jnp.dot(sb, wout_ref[:D_GROUP, :], preferred_element_type=F32)
        x = x + jnp.dot(dl, wout_ref[D_GROUP:, :], preferred_element_type=F32)
    gain_ref, wg_ref, wu_ref, wd_ref = refs[:4]
    h = (_rms(x) * gain_ref[...]).astype(BF16)
    acc = jnp.zeros(x.shape, F32)
    for c in range(0, D_FF, FF_CHUNK):
        g = jnp.dot(h, wg_ref[:, c:c + FF_CHUNK], preferred_element_type=F32)
        u = jnp.dot(h, wu_ref[:, c:c + FF_CHUNK], preferred_element_type=F32)
        a = (g * jax.nn.sigmoid(g) * u).astype(BF16)
        acc = acc + jnp.dot(a, wd_ref[c:c + FF_CHUNK, :], preferred_element_type=F32)
    y = x + HALF_STEP * acc
    if final_norm:
        y = _rms(y) * refs[4][...]
    o_ref[...] = y


def _ffn_block(x, gain, wg, wu, wd, mixer_out=None, final_gain=None, round_later=None):
    t = x.shape[0]
    n_steps = t // TM
    row = lambda w: pl.BlockSpec((TM, w), lambda i: (i, 0))
    vec = lambda w: _resident((1, w))
    in_specs, args = [row(D_MODEL)], [x]
    if mixer_out is not None:
        o_sb, o_dl, g_sb, g_dl, w_out = mixer_out
        in_specs += [row(D_GROUP), row(D_GROUP), vec(D_GROUP), vec(D_GROUP), _resident((D_MODEL, D_MODEL))]
        args += [o_sb, o_dl, g_sb.reshape(1, D_GROUP), g_dl.reshape(1, D_GROUP), w_out]
    in_specs += [vec(D_MODEL), _resident((D_MODEL, D_FF)), _resident((D_MODEL, D_FF)), _resident((D_FF, D_MODEL))]
    args += [gain.reshape(1, D_MODEL), wg, wu, wd]
    if final_gain is not None:
        in_specs.append(vec(D_MODEL))
        args.append(final_gain.reshape(1, D_MODEL))
    out_specs, out_shape = [row(D_MODEL)], [jax.ShapeDtypeStruct((t, D_MODEL), F32)]
    layer, carried = round_later if round_later is not None else (0, ())
    for w in carried:
        _, rows, cols = w.shape
        hold = next(k for k in range(1, n_steps + 1) if n_steps % k == 0 and rows % (n_steps // k * 16) == 0)
        blk = rows // (n_steps // hold)
        in_specs.append(pl.BlockSpec((None, blk, cols), lambda i, hold=hold: (layer, i // hold, 0)))
        out_specs.append(pl.BlockSpec((blk, cols), lambda i, hold=hold: (i // hold, 0)))
        out_shape.append(jax.ShapeDtypeStruct((rows, cols), BF16))
        args.append(w)
    out = pl.pallas_call(
        functools.partial(_ffn_kernel, mixer_out=mixer_out is not None, final_norm=final_gain is not None,
                          n_carried=len(carried)),
        grid=(n_steps,),
        in_specs=in_specs,
        out_specs=out_specs,
        out_shape=out_shape,
        compiler_params=_params("arbitrary" if carried else "parallel"),
        name="ffn_block",
    )(*args)
    return out if carried else out[0]


def _in_proj_kernel(x_ref, gain_ref, w_ref, cos_ref, sin_ref,
                    qsb_ref, ktsb_ref, vsb_ref, qdl_ref, kdl_ref, vdl_ref):
    h = (_rms(x_ref[...]) * gain_ref[...]).astype(BF16)

    def proj(j):
        return jnp.dot(h, w_ref[:, j * D_GROUP:(j + 1) * D_GROUP], preferred_element_type=F32)

    cos = cos_ref[...]
    sin = sin_ref[...]
    lane = lax.broadcasted_iota(jnp.int32, cos.shape, 1)
    low_half = (lane % HEAD_DIM) < HEAD_DIM // 2

    def rope(t):
        outs = []
        for j in range(D_GROUP // LANES):
            tj = t[:, j * LANES:(j + 1) * LANES]
            ahead = pltpu.roll(tj, LANES - HEAD_DIM // 2, 1)
            behind = pltpu.roll(tj, HEAD_DIM // 2, 1)
            outs.append(tj * cos + jnp.where(low_half, ahead, behind) * sin)
        return jnp.concatenate(outs, axis=1)

    qsb_ref[...] = (proj(0) * Q_SCALE).astype(BF16)
    kt = proj(1).T
    for hp in range(D_GROUP // LANES):
        for kb in range(PROJ_TM // SB_KB):
            ktsb_ref[hp, kb] = kt[hp * LANES:(hp + 1) * LANES, kb * SB_KB:(kb + 1) * SB_KB].astype(BF16)
    vsb_ref[...] = proj(2).astype(BF16)
    qdl_ref[...] = (rope(proj(3)) * Q_SCALE).astype(BF16)
    kdl_ref[...] = rope(proj(4)).astype(BF16)
    vdl_ref[...] = proj(5).astype(BF16)


def _in_proj(x, gain, w_in, cos, sin, batch, seq):
    t = x.shape[0]
    n_seq_tiles = seq // PROJ_TM
    row = lambda w: pl.BlockSpec((PROJ_TM, w), lambda b, i: (b * n_seq_tiles + i, 0))
    grp = jax.ShapeDtypeStruct((t, D_GROUP), BF16)
    kt_shape = (batch, D_GROUP // LANES, seq // SB_KB, LANES, SB_KB)
    return pl.pallas_call(
        _in_proj_kernel,
        grid=(batch, n_seq_tiles),
        in_specs=[row(D_MODEL), _resident((1, D_MODEL)), _resident((D_MODEL, D_IN)),
                  pl.BlockSpec((PROJ_TM, LANES), lambda b, i: (i, 0)),
                  pl.BlockSpec((PROJ_TM, LANES), lambda b, i: (i, 0))],
        out_specs=[row(D_GROUP),
                   pl.BlockSpec((None, D_GROUP // LANES, PROJ_TM // SB_KB, LANES, SB_KB),
                                lambda b, i: (b, 0, i, 0, 0)),
                   row(D_GROUP), row(D_GROUP), row(D_GROUP), row(D_GROUP)],
        out_shape=[grp, jax.ShapeDtypeStruct(kt_shape, BF16), grp, grp, grp, grp],
        compiler_params=_params("parallel", "parallel"),
        name="in_proj",
    )(x, gain.reshape(1, D_MODEL), w_in, cos, sin)


def _sb_kernel(q_ref, kt_ref, v_ref, csum_ref, o_ref, qs_ref, acc_ref, run_ref):
    i = pl.program_id(2)
    n_heads = LANES // HEAD_DIM
    lane = lax.broadcasted_iota(jnp.int32, (SB_TQ, LANES), 1)
    for hp in range(SB_PAIRS):
        q = q_ref[:, hp * LANES:(hp + 1) * LANES]
        for hh in range(n_heads):
            in_head = (lane >= hh * HEAD_DIM) & (lane < (hh + 1) * HEAD_DIM)
            qs_ref[hp, hh * SB_TQ:(hh + 1) * SB_TQ, :] = jnp.where(in_head, q, jnp.zeros_like(q))
    run_ref[...] = jnp.zeros_like(run_ref)
    acc_ref[...] = jnp.zeros_like(acc_ref)
    csum_w = csum_ref[...]

    def scores(hp, kb):
        return jnp.dot(qs_ref[hp], kt_ref[hp, kb], preferred_element_type=F32)

    def weights(hp, w, mask):
        p = jnp.maximum(w, 0.0) + jnp.log(1.0 + jnp.exp2(-jnp.abs(w))) * INV_LN2
        if mask is not None:
            p = jnp.where(mask, p, 0.0)
        neg_c = jnp.dot(p.astype(BF16), csum_w, preferred_element_type=F32)
        run = run_ref[hp]
        a = jnp.exp2((w - p) + neg_c + jnp.concatenate([run] * (SB_KB // LANES), axis=1))
        if mask is not None:
            a = jnp.where(mask, a, 0.0)
        run_ref[hp] = run + jnp.broadcast_to(neg_c[:, 0:1] - p[:, 0:1], run.shape)
        return a.astype(BF16)

    def accumulate(hp, a, kb):
        v = v_ref[pl.ds(pl.multiple_of(kb * SB_KB, SB_KB), SB_KB), hp * LANES:(hp + 1) * LANES]
        acc_ref[hp] += jnp.dot(a, v, preferred_element_type=F32)

    def block(kb, mask):
        for hp in range(SB_PAIRS):
            accumulate(hp, weights(hp, scores(hp, kb), mask), kb)

    row = lax.broadcasted_iota(jnp.int32, (n_heads * SB_TQ, SB_KB), 0) % SB_TQ
    col = lax.broadcasted_iota(jnp.int32, (n_heads * SB_TQ, SB_KB), 1)
    block(i, col < row)

    run_ref[...] = jnp.where(i > 0, run_ref[...], 2 * SB_DEAD_LOG2)
    block(jnp.maximum(i - 1, 0), None)

    def alive():
        return (jnp.max(run_ref[...]) > SB_DEAD_LOG2).astype(jnp.int32)

    def older(carry):
        kb, _ = carry
        block(kb, None)
        return kb - 1, alive()

    lax.while_loop(lambda c: (c[0] >= 0) & (c[1] > 0), older, (i - 2, alive()))
    for hp in range(SB_PAIRS):
        o_ref[:, hp * LANES:(hp + 1) * LANES] = jnp.where(lane < HEAD_DIM, acc_ref[hp, :SB_TQ, :], acc_ref[hp, SB_TQ:, :])


def _csum_weights():
    j = jnp.arange(SB_KB)[:, None]
    s = jnp.arange(SB_KB)[None, :]
    return jnp.where(j > s, -1.0, 0.0).astype(BF16)


def _sb_attention(q, kt, v, batch, seq):
    t = q.shape[0]
    n_q = seq // SB_TQ
    n_groups = D_GROUP // (SB_PAIRS * LANES)
    m = (LANES // HEAD_DIM) * SB_TQ
    return pl.pallas_call(
        _sb_kernel,
        grid=(batch, n_groups, n_q),
        in_specs=[pl.BlockSpec((SB_TQ, SB_PAIRS * LANES), lambda b, g, i: (b * n_q + i, g)),
                  pl.BlockSpec((None, SB_PAIRS, seq // SB_KB, LANES, SB_KB), lambda b, g, i: (b, g, 0, 0, 0)),
                  pl.BlockSpec((seq, SB_PAIRS * LANES), lambda b, g, i: (b, g)),
                  _resident((SB_KB, SB_KB))],
        out_specs=pl.BlockSpec((SB_TQ, SB_PAIRS * LANES), lambda b, g, i: (b * n_q + i, g)),
        out_shape=jax.ShapeDtypeStruct((t, D_GROUP), F32),
        scratch_shapes=[pltpu.VMEM((SB_PAIRS, m, LANES), BF16),
                        pltpu.VMEM((SB_PAIRS, m, LANES), F32),
                        pltpu.VMEM((SB_PAIRS, m, LANES), F32)],
        compiler_params=_params("parallel", "parallel", "arbitrary"),
        name="sb_attention",
    )(q, kt, v, _csum_weights())


def _dilated_kernel(q_ref, kp_ref, kc_ref, vp_ref, vc_ref, o_ref,
                    qn_ref, kn_ref, vn_ref, q4_ref, k4_ref, v4_ref,
                    on4_ref, mn4_ref, ln4_ref, on16_ref, mn16_ref, ln16_ref):
    st = pl.program_id(2)
    n_heads = LANES // HEAD_DIM
    lane = lax.broadcasted_iota(jnp.int32, (DL_TQ, LANES), 1)
    row = lax.broadcasted_iota(jnp.int32, (DL_TQ, 2 * DL_TQ), 0)
    col = lax.broadcasted_iota(jnp.int32, (DL_TQ, 2 * DL_TQ), 1)
    band = (col >= row) & (col <= row + DL_TQ)
    band_edge = band & (col >= jnp.where(st == 0, DL_TQ, 0))
    head_lanes = [(lane >= hh * HEAD_DIM) & (lane < (hh + 1) * HEAD_DIM) for hh in range(n_heads)]

    qn_ref[...] = q_ref[...].astype(F32)
    kn_ref[:DL_SPAN, :] = kp_ref[...].astype(F32)
    kn_ref[DL_SPAN:, :] = kc_ref[...].astype(F32)
    vn_ref[:DL_SPAN, :] = vp_ref[...].astype(F32)
    vn_ref[DL_SPAN:, :] = vc_ref[...].astype(F32)

    def attend(q, k, v, valid):
        qs = jnp.concatenate([jnp.where(head_lanes[hh], q, jnp.zeros_like(q)) for hh in range(n_heads)], axis=0)
        z = lax.dot_general(qs, k, (((1,), (1,)), ((), ())), preferred_element_type=F32)
        z = jnp.where(jnp.concatenate([valid] * n_heads, axis=0), z, -jnp.inf)
        m = jnp.max(z, axis=-1, keepdims=True)
        p = jnp.exp2(z - m)
        den = jnp.sum(p, axis=-1, keepdims=True)
        num = jnp.dot(p.astype(BF16), v, preferred_element_type=F32)
        m, den = jnp.broadcast_to(m, num.shape), jnp.broadcast_to(den, num.shape)
        return tuple(jnp.where(lane < HEAD_DIM, t[:DL_TQ], t[DL_TQ:]) for t in (num, m, den))

    for src, dst in ((qn_ref, q4_ref), (kn_ref, k4_ref), (vn_ref, v4_ref)):
        per_class = src.shape[0] // 4
        for c in range(4):
            dst[c * per_class:(c + 1) * per_class, :] = src[pl.ds(c, per_class, stride=4), :]

    def strided_pattern(d, on_ref, mn_ref, ln_ref):
        sub = d // 4
        q_per_class, k_per_class = DL_SPAN // 4, 2 * DL_SPAN // 4
        for c in range(d):
            c4, r = c % 4, c // 4
            for j in range(DL_SPAN // (d * DL_TQ)):
                q0 = c + d * DL_TQ * j
                qi = r + sub * DL_TQ * j
                ki = k_per_class // 2 + qi - sub * DL_TQ
                q = q4_ref[pl.ds(c4 * q_per_class + qi, DL_TQ, stride=sub), :].astype(BF16)
                k = k4_ref[pl.ds(c4 * k_per_class + ki, 2 * DL_TQ, stride=sub), :].astype(BF16)
                v = v4_ref[pl.ds(c4 * k_per_class + ki, 2 * DL_TQ, stride=sub), :].astype(BF16)
                num, m, den = attend(q, k, v, band_edge if j == 0 else band)
                on_ref[pl.ds(q0, DL_TQ, stride=d), :] = num
                mn_ref[pl.ds(q0, DL_TQ, stride=d), :] = m
                ln_ref[pl.ds(q0, DL_TQ, stride=d), :] = den

    strided_pattern(16, on16_ref, mn16_ref, ln16_ref)
    strided_pattern(4, on4_ref, mn4_ref, ln4_ref)

    def mix(part1, r0):
        rows = pl.ds(r0, DL_TQ)
        parts = (part1, (on4_ref[rows, :], mn4_ref[rows, :], ln4_ref[rows, :]),
                 (on16_ref[rows, :], mn16_ref[rows, :], ln16_ref[rows, :]))
        top = functools.reduce(jnp.maximum, [m for _, m, _ in parts])
        scale = [jnp.exp2(m - top) for _, m, _ in parts]
        num = sum(s * n for s, (n, _, _) in zip(scale, parts))
        den = sum(s * d for s, (_, _, d) in zip(scale, parts))
        return num / den

    k_edge = jnp.concatenate([kp_ref[DL_SPAN - DL_TQ:, :], kc_ref[:DL_TQ, :]], axis=0)
    v_edge = jnp.concatenate([vp_ref[DL_SPAN - DL_TQ:, :], vc_ref[:DL_TQ, :]], axis=0)
    o_ref[:DL_TQ, :] = mix(attend(q_ref[:DL_TQ, :], k_edge, v_edge, band_edge), 0)

    for j in range(1, DL_SPAN // DL_TQ):
        q0, k0 = j * DL_TQ, (j - 1) * DL_TQ
        part = attend(q_ref[q0:q0 + DL_TQ, :], kc_ref[k0:k0 + 2 * DL_TQ, :], vc_ref[k0:k0 + 2 * DL_TQ, :], band)
        o_ref[q0:q0 + DL_TQ, :] = mix(part, q0)


def _dilated_attention(q, k, v, batch, seq):
    t = q.shape[0]
    assert seq % DL_SPAN == 0
    n_span = seq // DL_SPAN
    cur = pl.BlockSpec((DL_SPAN, LANES), lambda b, hp, st: (b * n_span + st, hp))
    prev = pl.BlockSpec((DL_SPAN, LANES), lambda b, hp, st: (b * n_span + jnp.maximum(st - 1, 0), hp))
    nat = pltpu.VMEM((DL_SPAN, LANES), F32)
    both = pltpu.VMEM((2 * DL_SPAN, LANES), F32)
    return pl.pallas_call(
        _dilated_kernel,
        grid=(batch, D_GROUP // LANES, n_span),
        in_specs=[cur, prev, cur, prev, cur],
        out_specs=cur,
        out_shape=jax.ShapeDtypeStruct((t, D_GROUP), F32),
        scratch_shapes=[nat, both, both, nat, both, both, nat, nat, nat, nat, nat, nat],
        compiler_params=_params("parallel", "parallel", "arbitrary"),
        name="dilated_mixture",
    )(q, k, k, v, v)


def _cast_kernel(*refs):
    n = len(refs) // 2
    for src, dst in zip(refs[:n], refs[n:]):
        dst[...] = src[...].astype(dst.dtype)


def _layer_weights_bf16(layer, *stacked):
    in_specs, out_specs, out_shape = [], [], []
    for w in stacked:
        _, rows, cols = w.shape
        assert rows % (CAST_STEPS * 16) == 0
        in_specs.append(pl.BlockSpec((None, rows // CAST_STEPS, cols), lambda i: (layer, i, 0)))
        out_specs.append(pl.BlockSpec((rows // CAST_STEPS, cols), lambda i: (i, 0)))
        out_shape.append(jax.ShapeDtypeStruct((rows, cols), BF16))
    return pl.pallas_call(
        _cast_kernel,
        grid=(CAST_STEPS,),
        in_specs=in_specs,
        out_specs=out_specs,
        out_shape=out_shape,
        compiler_params=_params("parallel"),
        name="weights_bf16",
    )(*stacked)


def _rope_tables(seq):
    half = HEAD_DIM // 2
    inv_freq = ROPE_THETA ** (-np.arange(half, dtype=np.float64) / half)
    ang = np.arange(seq, dtype=np.float64)[:, None] * inv_freq[None, :]
    cos, sin = np.cos(ang), np.sin(ang)
    reps = LANES // HEAD_DIM
    cos_t = np.tile(np.concatenate([cos, cos], axis=1), (1, reps))
    sin_t = np.tile(np.concatenate([-sin, sin], axis=1), (1, reps))
    return jnp.asarray(cos_t, F32), jnp.asarray(sin_t, F32)


def kernel(x, ffn1_norm, ffn1_w_gate, ffn1_w_up, ffn1_w_down, mix_norm, w_in, sb_out_norm, dil_out_norm, w_out, ffn2_norm, ffn2_w_gate, ffn2_w_up, ffn2_w_down, final_norm):
    batch, seq, _ = x.shape
    depth = ffn1_norm.shape[0]
    cos, sin = _rope_tables(seq)
    h = x.reshape(batch * seq, D_MODEL)
    for layer in range(depth):
        g1, u1, d1, w_qkv = _layer_weights_bf16(layer, ffn1_w_gate, ffn1_w_up, ffn1_w_down, w_in)
        h, w_o, g2, u2, d2 = _ffn_block(h, ffn1_norm[layer], g1, u1, d1,
                                        round_later=(layer, (w_out, ffn2_w_gate, ffn2_w_up, ffn2_w_down)))
        q_sb, kt_sb, v_sb, q_dl, k_dl, v_dl = _in_proj(h, mix_norm[layer], w_qkv, cos, sin, batch, seq)
        o_sb = _sb_attention(q_sb, kt_sb, v_sb, batch, seq)
        o_dl = _dilated_attention(q_dl, k_dl, v_dl, batch, seq)
        h = _ffn_block(h, ffn2_norm[layer], g2, u2, d2,
                       mixer_out=(o_sb, o_dl, sb_out_norm[layer], dil_out_norm[layer], w_o),
                       final_gain=final_norm if layer == depth - 1 else None)
    return h.reshape(batch, seq, D_MODEL)
```

```python
import functools
import math

import jax
import jax.numpy as jnp
import numpy as np
from jax import lax
from jax.experimental import pallas as pl
from jax.experimental.pallas import tpu as pltpu

D_MODEL = 1024
HEAD_DIM = 64
D_GROUP = 512
D_IN = 6 * D_GROUP
D_FF = 2816
DILATED_PATTERNS = ((128, 1), (512, 4), (2048, 16))
ROPE_THETA = 10000.0
RMS_EPS = 1e-6
HALF_STEP = 0.5

LANES = 128
LOG2E = math.log2(math.e)
INV_LN2 = 1.0 / math.log(2.0)
Q_SCALE = HEAD_DIM ** -0.5 * LOG2E

TM = 512
PROJ_TM = 1024
FF_CHUNK = 256
SB_TQ = 256
SB_PAIRS = 4
SB_KB = 256
SB_DEAD_LOG2 = -160.0
assert SB_TQ == SB_KB
DL_TQ = 128
DL_SPAN = DL_TQ * max(d for _, d in DILATED_PATTERNS)
assert [d for _, d in DILATED_PATTERNS] == [1, 4, 16] and all(w // d == DL_TQ for w, d in DILATED_PATTERNS)
CAST_STEPS = 8
VMEM_LIMIT = 48 * 1024 * 1024

F32 = jnp.float32
BF16 = jnp.bfloat16


def _rms(x):
    return x * lax.rsqrt(jnp.mean(x * x, axis=-1, keepdims=True) + RMS_EPS)


def _params(*sem):
    return pltpu.CompilerParams(dimension_semantics=sem, vmem_limit_bytes=VMEM_LIMIT)


def _resident(shape):
    return pl.BlockSpec(shape, lambda *_: (0,) * len(shape), pipeline_mode=pl.Buffered(1))


def _ffn_kernel(*refs, mixer_out, final_norm, n_carried):
    refs = list(refs)
    carried_out = [refs.pop() for _ in range(n_carried)][::-1]
    o_ref = refs.pop()
    carried_in = [refs.pop() for _ in range(n_carried)][::-1]
    for src, dst in zip(carried_in, carried_out):
        dst[...] = src[...].astype(dst.dtype)
    x = refs.pop(0)[...]
    if mixer_out:
        osb_ref, odl_ref, gsb_ref, gdl_ref, wout_ref = refs[:5]
        del refs[:5]
        sb = (_rms(osb_ref[...]) * gsb_ref[...]).astype(BF16)
        dl = (_rms(odl_ref[...]) * gdl_ref[...]).astype(BF16)
        x = x + jnp.dot(sb, wout_ref[:D_GROUP, :], preferred_element_type=F32)
        x = x + jnp.dot(dl, wout_ref[D_GROUP:, :], preferred_element_type=F32)
    gain_ref, wg_ref, wu_ref, wd_ref = refs[:4]
    h = (_rms(x) * gain_ref[...]).astype(BF16)
    acc = jnp.zeros(x.shape, F32)
    for c in range(0, D_FF, FF_CHUNK):
        g = jnp.dot(h, wg_ref[:, c:c + FF_CHUNK], preferred_element_type=F32)
        u = jnp.dot(h, wu_ref[:, c:c + FF_CHUNK], preferred_element_type=F32)
        a = (g * jax.nn.sigmoid(g) * u).astype(BF16)
        acc = acc + jnp.dot(a, wd_ref[c:c + FF_CHUNK, :], preferred_element_type=F32)
    y = x + HALF_STEP * acc
    if final_norm:
        y = _rms(y) * refs[4][...]
    o_ref[...] = y


def _ffn_block(x, gain, wg, wu, wd, mixer_out=None, final_gain=None, round_later=None):
    t = x.shape[0]
    n_steps = t // TM
    row = lambda w: pl.BlockSpec((TM, w), lambda i: (i, 0))
    vec = lambda w: _resident((1, w))
    in_specs, args = [row(D_MODEL)], [x]
    if mixer_out is not None:
        o_sb, o_dl, g_sb, g_dl, w_out = mixer_out
        in_specs += [row(D_GROUP), row(D_GROUP), vec(D_GROUP), vec(D_GROUP), _resident((D_MODEL, D_MODEL))]
        args += [o_sb, o_dl, g_sb.reshape(1, D_GROUP), g_dl.reshape(1, D_GROUP), w_out]
    in_specs += [vec(D_MODEL), _resident((D_MODEL, D_FF)), _resident((D_MODEL, D_FF)), _resident((D_FF, D_MODEL))]
    args += [gain.reshape(1, D_MODEL), wg, wu, wd]
    if final_gain is not None:
        in_specs.append(vec(D_MODEL))
        args.append(final_gain.reshape(1, D_MODEL))
    out_specs, out_shape = [row(D_MODEL)], [jax.ShapeDtypeStruct((t, D_MODEL), F32)]
    layer, carried = round_later if round_later is not None else (0, ())
    for w in carried:
        _, rows, cols = w.shape
        hold = next(k for k in range(1, n_steps + 1) if n_steps % k == 0 and rows % (n_steps // k * 16) == 0)
        blk = rows // (n_steps // hold)
        in_specs.append(pl.BlockSpec((None, blk, cols), lambda i, hold=hold: (layer, i // hold, 0)))
        out_specs.append(pl.BlockSpec((blk, cols), lambda i, hold=hold: (i // hold, 0)))
        out_shape.append(jax.ShapeDtypeStruct((rows, cols), BF16))
        args.append(w)
    out = pl.pallas_call(
        functools.partial(_ffn_kernel, mixer_out=mixer_out is not None, final_norm=final_gain is not None,
                          n_carried=len(carried)),
        grid=(n_steps,),
        in_specs=in_specs,
        out_specs=out_specs,
        out_shape=out_shape,
        compiler_params=_params("arbitrary" if carried else "parallel"),
        name="ffn_block",
    )(*args)
    return out if carried else out[0]


def _in_proj_kernel(x_ref, gain_ref, w_ref, cos_ref, sin_ref,
                    qsb_ref, ktsb_ref, vsb_ref, qdl_ref, kdl_ref, vdl_ref):
    h = (_rms(x_ref[...]) * gain_ref[...]).astype(BF16)

    def proj(j):
        return jnp.dot(h, w_ref[:, j * D_GROUP:(j + 1) * D_GROUP], preferred_element_type=F32)

    cos = cos_ref[...]
    sin = sin_ref[...]
    lane = lax.broadcasted_iota(jnp.int32, cos.shape, 1)
    low_half = (lane % HEAD_DIM) < HEAD_DIM // 2

    def rope(t):
        outs = []
        for j in range(D_GROUP // LANES):
            tj = t[:, j * LANES:(j + 1) * LANES]
            ahead = pltpu.roll(tj, LANES - HEAD_DIM // 2, 1)
            behind = pltpu.roll(tj, HEAD_DIM // 2, 1)
            outs.append(tj * cos + jnp.where(low_half, ahead, behind) * sin)
        return jnp.concatenate(outs, axis=1)

    qsb_ref[...] = (proj(0) * Q_SCALE).astype(BF16)
    kt = proj(1).T
    for hp in range(D_GROUP // LANES):
        for kb in range(PROJ_TM // SB_KB):
            ktsb_ref[hp, kb] = kt[hp * LANES:(hp + 1) * LANES, kb * SB_KB:(kb + 1) * SB_KB].astype(BF16)
    vsb_ref[...] = proj(2).astype(BF16)
    qdl_ref[...] = (rope(proj(3)) * Q_SCALE).astype(BF16)
    kdl_ref[...] = rope(proj(4)).astype(BF16)
    vdl_ref[...] = proj(5).astype(BF16)


def _in_proj(x, gain, w_in, cos, sin, batch, seq):
    t = x.shape[0]
    n_seq_tiles = seq // PROJ_TM
    row = lambda w: pl.BlockSpec((PROJ_TM, w), lambda b, i: (b * n_seq_tiles + i, 0))
    grp = jax.ShapeDtypeStruct((t, D_GROUP), BF16)
    kt_shape = (batch, D_GROUP // LANES, seq // SB_KB, LANES, SB_KB)
    return pl.pallas_call(
        _in_proj_kernel,
        grid=(batch, n_seq_tiles),
        in_specs=[row(D_MODEL), _resident((1, D_MODEL)), _resident((D_MODEL, D_IN)),
                  pl.BlockSpec((PROJ_TM, LANES), lambda b, i: (i, 0)),
                  pl.BlockSpec((PROJ_TM, LANES), lambda b, i: (i, 0))],
        out_specs=[row(D_GROUP),
                   pl.BlockSpec((None, D_GROUP // LANES, PROJ_TM // SB_KB, LANES, SB_KB),
                                lambda b, i: (b, 0, i, 0, 0)),
                   row(D_GROUP), row(D_GROUP), row(D_GROUP), row(D_GROUP)],
        out_shape=[grp, jax.ShapeDtypeStruct(kt_shape, BF16), grp, grp, grp, grp],
        compiler_params=_params("parallel", "parallel"),
        name="in_proj",
    )(x, gain.reshape(1, D_MODEL), w_in, cos, sin)


def _sb_kernel(q_ref, kt_ref, v_ref, csum_ref, o_ref, qs_ref, acc_ref, run_ref):
    i = pl.program_id(2)
    n_heads = LANES // HEAD_DIM
    lane = lax.broadcasted_iota(jnp.int32, (SB_TQ, LANES), 1)
    for hp in range(SB_PAIRS):
        q = q_ref[:, hp * LANES:(hp + 1) * LANES]
        for hh in range(n_heads):
            in_head = (lane >= hh * HEAD_DIM) & (lane < (hh + 1) * HEAD_DIM)
            qs_ref[hp, hh * SB_TQ:(hh + 1) * SB_TQ, :] = jnp.where(in_head, q, jnp.zeros_like(q))
    run_ref[...] = jnp.zeros_like(run_ref)
    acc_ref[...] = jnp.zeros_like(acc_ref)
    csum_w = csum_ref[...]

    def scores(hp, kb):
        return jnp.dot(qs_ref[hp], kt_ref[hp, kb], preferred_element_type=F32)

    def weights(hp, w, mask):
        p = jnp.maximum(w, 0.0) + jnp.log(1.0 + jnp.exp2(-jnp.abs(w))) * INV_LN2
        if mask is not None:
            p = jnp.where(mask, p, 0.0)
        neg_c = jnp.dot(p.astype(BF16), csum_w, preferred_element_type=F32)
        run = run_ref[hp]
        a = jnp.exp2((w - p) + neg_c + jnp.concatenate([run] * (SB_KB // LANES), axis=1))
        if mask is not None:
            a = jnp.where(mask, a, 0.0)
        run_ref[hp] = run + jnp.broadcast_to(neg_c[:, 0:1] - p[:, 0:1], run.shape)
        return a.astype(BF16)

    def accumulate(hp, a, kb):
        v = v_ref[pl.ds(pl.multiple_of(kb * SB_KB, SB_KB), SB_KB), hp * LANES:(hp + 1) * LANES]
        acc_ref[hp] += jnp.dot(a, v, preferred_element_type=F32)

    def block(kb, mask):
        for hp in range(SB_PAIRS):
            accumulate(hp, weights(hp, scores(hp, kb), mask), kb)

    row = lax.broadcasted_iota(jnp.int32, (n_heads * SB_TQ, SB_KB), 0) % SB_TQ
    col = lax.broadcasted_iota(jnp.int32, (n_heads * SB_TQ, SB_KB), 1)
    block(i, col < row)

    run_ref[...] = jnp.where(i > 0, run_ref[...], 2 * SB_DEAD_LOG2)
    block(jnp.maximum(i - 1, 0), None)

    def alive():
        return (jnp.max(run_ref[...]) > SB_DEAD_LOG2).astype(jnp.int32)

    def older(carry):
        kb, _ = carry
        block(kb, None)
        return kb - 1, alive()

    lax.while_loop(lambda c: (c[0] >= 0) & (c[1] > 0), older, (i - 2, alive()))
    for hp in range(SB_PAIRS):
        o_ref[:, hp * LANES:(hp + 1) * LANES] = jnp.where(lane < HEAD_DIM, acc_ref[hp, :SB_TQ, :], acc_ref[hp, SB_TQ:, :])


def _csum_weights():
    j = jnp.arange(SB_KB)[:, None]
    s = jnp.arange(SB_KB)[None, :]
    return jnp.where(j > s, -1.0, 0.0).astype(BF16)


def _sb_attention(q, kt, v, batch, seq):
    t = q.shape[0]
    n_q = seq // SB_TQ
    n_groups = D_GROUP // (SB_PAIRS * LANES)
    m = (LANES // HEAD_DIM) * SB_TQ
    return pl.pallas_call(
        _sb_kernel,
        grid=(batch, n_groups, n_q),
        in_specs=[pl.BlockSpec((SB_TQ, SB_PAIRS * LANES), lambda b, g, i: (b * n_q + i, g)),
                  pl.BlockSpec((None, SB_PAIRS, seq // SB_KB, LANES, SB_KB), lambda b, g, i: (b, g, 0, 0, 0)),
                  pl.BlockSpec((seq, SB_PAIRS * LANES), lambda b, g, i: (b, g)),
                  _resident((SB_KB, SB_KB))],
        out_specs=pl.BlockSpec((SB_TQ, SB_PAIRS * LANES), lambda b, g, i: (b * n_q + i, g)),
        out_shape=jax.ShapeDtypeStruct((t, D_GROUP), F32),
        scratch_shapes=[pltpu.VMEM((SB_PAIRS, m, LANES), BF16),
                        pltpu.VMEM((SB_PAIRS, m, LANES), F32),
                        pltpu.VMEM((SB_PAIRS, m, LANES), F32)],
        compiler_params=_params("parallel", "parallel", "arbitrary"),
        name="sb_attention",
    )(q, kt, v, _csum_weights())


def _dilated_kernel(q_ref, kp_ref, kc_ref, vp_ref, vc_ref, o_ref,
                    qn_ref, kn_ref, vn_ref, q4_ref, k4_ref, v4_ref,
                    on4_ref, mn4_ref, ln4_ref, on16_ref, mn16_ref, ln16_ref):
    st = pl.program_id(2)
    n_heads = LANES // HEAD_DIM
    lane = lax.broadcasted_iota(jnp.int32, (DL_TQ, LANES), 1)
    row = lax.broadcasted_iota(jnp.int32, (DL_TQ, 2 * DL_TQ), 0)
    col = lax.broadcasted_iota(jnp.int32, (DL_TQ, 2 * DL_TQ), 1)
    band = (col >= row) & (col <= row + DL_TQ)
    band_edge = band & (col >= jnp.where(st == 0, DL_TQ, 0))
    head_lanes = [(lane >= hh * HEAD_DIM) & (lane < (hh + 1) * HEAD_DIM) for hh in range(n_heads)]

    qn_ref[...] = q_ref[...].astype(F32)
    kn_ref[:DL_SPAN, :] = kp_ref[...].astype(F32)
    kn_ref[DL_SPAN:, :] = kc_ref[...].astype(F32)
    vn_ref[:DL_SPAN, :] = vp_ref[...].astype(F32)
    vn_ref[DL_SPAN:, :] = vc_ref[...].astype(F32)

    def attend(q, k, v, valid):
        qs = jnp.concatenate([jnp.where(head_lanes[hh], q, jnp.zeros_like(q)) for hh in range(n_heads)], axis=0)
        z = lax.dot_general(qs, k, (((1,), (1,)), ((), ())), preferred_element_type=F32)
        z = jnp.where(jnp.concatenate([valid] * n_heads, axis=0), z, -jnp.inf)
        m = jnp.max(z, axis=-1, keepdims=True)
        p = jnp.exp2(z - m)
        den = jnp.sum(p, axis=-1, keepdims=True)
        num = jnp.dot(p.astype(BF16), v, preferred_element_type=F32)
        m, den = jnp.broadcast_to(m, num.shape), jnp.broadcast_to(den, num.shape)
        return tuple(jnp.where(lane < HEAD_DIM, t[:DL_TQ], t[DL_TQ:]) for t in (num, m, den))

    for src, dst in ((qn_ref, q4_ref), (kn_ref, k4_ref), (vn_ref, v4_ref)):
        per_class = src.shape[0] // 4
        for c in range(4):
            dst[c * per_class:(c + 1) * per_class, :] = src[pl.ds(c, per_class, stride=4), :]

    def strided_pattern(d, on_ref, mn_ref, ln_ref):
        sub = d // 4
        q_per_class, k_per_class = DL_SPAN // 4, 2 * DL_SPAN // 4
        for c in range(d):
            c4, r = c % 4, c // 4
            for j in range(DL_SPAN // (d * DL_TQ)):
                q0 = c + d * DL_TQ * j
                qi = r + sub * DL_TQ * j
                ki = k_per_class // 2 + qi - sub * DL_TQ
                q = q4_ref[pl.ds(c4 * q_per_class + qi, DL_TQ, stride=sub), :].astype(BF16)
                k = k4_ref[pl.ds(c4 * k_per_class + ki, 2 * DL_TQ, stride=sub), :].astype(BF16)
                v = v4_ref[pl.ds(c4 * k_per_class + ki, 2 * DL_TQ, stride=sub), :].astype(BF16)
                num, m, den = attend(q, k, v, band_edge if j == 0 else band)
                on_ref[pl.ds(q0, DL_TQ, stride=d), :] = num
                mn_ref[pl.ds(q0, DL_TQ, stride=d), :] = m
                ln_ref[pl.ds(q0, DL_TQ, stride=d), :] = den

    strided_pattern(16, on16_ref, mn16_ref, ln16_ref)
    strided_pattern(4, on4_ref, mn4_ref, ln4_ref)

    def mix(part1, r0):
        rows = pl.ds(r0, DL_TQ)
        parts = (part1, (on4_ref[rows, :], mn4_ref[rows, :], ln4_ref[rows, :]),
                 (on16_ref[rows, :], mn16_ref[rows, :], ln16_ref[rows, :]))
        top = functools.reduce(jnp.maximum, [m for _, m, _ in parts])
        scale = [jnp.exp2(m - top) for _, m, _ in parts]
        num = sum(s * n for s, (n, _, _) in zip(scale, parts))
        den = sum(s * d for s, (_, _, d) in zip(scale, parts))
        return num / den

    k_edge = jnp.concatenate([kp_ref[DL_SPAN - DL_TQ:, :], kc_ref[:DL_TQ, :]], axis=0)
    v_edge = jnp.concatenate([vp_ref[DL_SPAN - DL_TQ:, :], vc_ref[:DL_TQ, :]], axis=0)
    o_ref[:DL_TQ, :] = mix(attend(q_ref[:DL_TQ, :], k_edge, v_edge, band_edge), 0)

    for j in range(1, DL_SPAN // DL_TQ):
        q0, k0 = j * DL_TQ, (j - 1) * DL_TQ
        part = attend(q_ref[q0:q0 + DL_TQ, :], kc_ref[k0:k0 + 2 * DL_TQ, :], vc_ref[k0:k0 + 2 * DL_TQ, :], band)
        o_ref[q0:q0 + DL_TQ, :] = mix(part, q0)


def _dilated_attention(q, k, v, batch, seq):
    t = q.shape[0]
    assert seq % DL_SPAN == 0
    n_span = seq // DL_SPAN
    cur = pl.BlockSpec((DL_SPAN, LANES), lambda b, hp, st: (b * n_span + st, hp))
    prev = pl.BlockSpec((DL_SPAN, LANES), lambda b, hp, st: (b * n_span + jnp.maximum(st - 1, 0), hp))
    nat = pltpu.VMEM((DL_SPAN, LANES), F32)
    both = pltpu.VMEM((2 * DL_SPAN, LANES), F32)
    return pl.pallas_call(
        _dilated_kernel,
        grid=(batch, D_GROUP // LANES, n_span),
        in_specs=[cur, prev, cur, prev, cur],
        out_specs=cur,
        out_shape=jax.ShapeDtypeStruct((t, D_GROUP), F32),
        scratch_shapes=[nat, both, both, nat, both, both, nat, nat, nat, nat, nat, nat],
        compiler_params=_params("parallel", "parallel", "arbitrary"),
        name="dilated_mixture",
    )(q, k, k, v, v)


def _cast_kernel(*refs):
    n = len(refs) // 2
    for src, dst in zip(refs[:n], refs[n:]):
        dst[...] = src[...].astype(dst.dtype)


def _layer_weights_bf16(layer, *stacked):
    in_specs, out_specs, out_shape = [], [], []
    for w in stacked:
        _, rows, cols = w.shape
        assert rows % (CAST_STEPS * 16) == 0
        in_specs.append(pl.BlockSpec((None, rows // CAST_STEPS, cols), lambda i: (layer, i, 0)))
        out_specs.append(pl.BlockSpec((rows // CAST_STEPS, cols), lambda i: (i, 0)))
        out_shape.append(jax.ShapeDtypeStruct((rows, cols), BF16))
    return pl.pallas_call(
        _cast_kernel,
        grid=(CAST_STEPS,),
        in_specs=in_specs,
        out_specs=out_specs,
        out_shape=out_shape,
        compiler_params=_params("parallel"),
        name="weights_bf16",
    )(*stacked)


def _rope_tables(seq):
    half = HEAD_DIM // 2
    inv_freq = ROPE_THETA ** (-np.arange(half, dtype=np.float64) / half)
    ang = np.arange(seq, dtype=np.float64)[:, None] * inv_freq[None, :]
    cos, sin = np.cos(ang), np.sin(ang)
    reps = LANES // HEAD_DIM
    cos_t = np.tile(np.concatenate([cos, cos], axis=1), (1, reps))
    sin_t = np.tile(np.concatenate([-sin, sin], axis=1), (1, reps))
    return jnp.asarray(cos_t, F32), jnp.asarray(sin_t, F32)


def kernel(x, ffn1_norm, ffn1_w_gate, ffn1_w_up, ffn1_w_down, mix_norm, w_in, sb_out_norm, dil_out_norm, w_out, ffn2_norm, ffn2_w_gate, ffn2_w_up, ffn2_w_down, final_norm):
    batch, seq, _ = x.shape
    depth = ffn1_norm.shape[0]
    cos, sin = _rope_tables(seq)
    h = x.reshape(batch * seq, D_MODEL)
    for layer in range(depth):
        g1, u1, d1 = _layer_weights_bf16(layer, ffn1_w_gate, ffn1_w_up, ffn1_w_down)
        h, w_qkv, w_o, g2, u2, d2 = _ffn_block(
            h, ffn1_norm[layer], g1, u1, d1,
            round_later=(layer, (w_in, w_out, ffn2_w_gate, ffn2_w_up, ffn2_w_down)))
        q_sb, kt_sb, v_sb, q_dl, k_dl, v_dl = _in_proj(h, mix_norm[layer], w_qkv, cos, sin, batch, seq)
        o_sb = _sb_attention(q_sb, kt_sb, v_sb, batch, seq)
        o_dl = _dilated_attention(q_dl, k_dl, v_dl, batch, seq)
        h = _ffn_block(h, ffn2_norm[layer], g2, u2, d2,
                       mixer_out=(o_sb, o_dl, sb_out_norm[layer], dil_out_norm[layer], w_o),
                       final_gain=final_norm if layer == depth - 1 else None)
    return h.reshape(batch, seq, D_MODEL)
```

```python
import functools
import math

import jax
import jax.numpy as jnp
import numpy as np
from jax import lax
from jax.experimental import pallas as pl
from jax.experimental.pallas import tpu as pltpu

D_MODEL = 1024
HEAD_DIM = 64
D_GROUP = 512
D_IN = 6 * D_GROUP
D_FF = 2816
DILATED_PATTERNS = ((128, 1), (512, 4), (2048, 16))
ROPE_THETA = 10000.0
RMS_EPS = 1e-6
HALF_STEP = 0.5

LANES = 128
LOG2E = math.log2(math.e)
INV_LN2 = 1.0 / math.log(2.0)
Q_SCALE = HEAD_DIM ** -0.5 * LOG2E

TM = 512
PROJ_TM = 1024
FF_CHUNK = 256
SB_TQ = 256
SB_PAIRS = 4
SB_KB = 256
SB_DEAD_LOG2 = -160.0
assert SB_TQ == SB_KB
DL_TQ = 128
DL_SPAN = DL_TQ * max(d for _, d in DILATED_PATTERNS)
assert [d for _, d in DILATED_PATTERNS] == [1, 4, 16] and all(w // d == DL_TQ for w, d in DILATED_PATTERNS)
CAST_STEPS = 8
VMEM_LIMIT = 48 * 1024 * 1024

F32 = jnp.float32
BF16 = jnp.bfloat16


def _rms(x):
    return x * lax.rsqrt(jnp.mean(x * x, axis=-1, keepdims=True) + RMS_EPS)


def _params(*sem):
    return pltpu.CompilerParams(dimension_semantics=sem, vmem_limit_bytes=VMEM_LIMIT)


def _resident(shape):
    return pl.BlockSpec(shape, lambda *_: (0,) * len(shape), pipeline_mode=pl.Buffered(1))


def _ffn_kernel(*refs, mixer_out, final_norm, n_carried):
    refs = list(refs)
    carried_out = [refs.pop() for _ in range(n_carried)][::-1]
    o_ref = refs.pop()
    carried_in = [refs.pop() for _ in range(n_carried)][::-1]
    for src, dst in zip(carried_in, carried_out):
        dst[...] = src[...].astype(dst.dtype)
    x = refs.pop(0)[...]
    if mixer_out:
        osb_ref, odl_ref, gsb_ref, gdl_ref, wout_ref = refs[:5]
        del refs[:5]
        sb = (_rms(osb_ref[...]) * gsb_ref[...]).astype(BF16)
        dl = (_rms(odl_ref[...]) * gdl_ref[...]).astype(BF16)
        x = x + jnp.dot(sb, wout_ref[:D_GROUP, :], preferred_element_type=F32)
        x = x + jnp.dot(dl, wout_ref[D_GROUP:, :], preferred_element_type=F32)
    gain_ref, wg_ref, wu_ref, wd_ref = refs[:4]
    h = (_rms(x) * gain_ref[...]).astype(BF16)
    acc = jnp.zeros(x.shape, F32)
    for c in range(0, D_FF, FF_CHUNK):
        g = jnp.dot(h, wg_ref[:, c:c + FF_CHUNK], preferred_element_type=F32)
        u = jnp.dot(h, wu_ref[:, c:c + FF_CHUNK], preferred_element_type=F32)
        a = (g * jax.nn.sigmoid(g) * u).astype(BF16)
        acc = acc + jnp.dot(a, wd_ref[c:c + FF_CHUNK, :], preferred_element_type=F32)
    y = x + HALF_STEP * acc
    if final_norm:
        y = _rms(y) * refs[4][...]
    o_ref[...] = y


def _ffn_block(x, gain, wg, wu, wd, mixer_out=None, final_gain=None, round_later=None):
    t = x.shape[0]
    n_steps = t // TM
    row = lambda w: pl.BlockSpec((TM, w), lambda i: (i, 0))
    vec = lambda w: _resident((1, w))
    in_specs, args = [row(D_MODEL)], [x]
    if mixer_out is not None:
        o_sb, o_dl, g_sb, g_dl, w_out = mixer_out
        in_specs += [row(D_GROUP), row(D_GROUP), vec(D_GROUP), vec(D_GROUP), _resident((D_MODEL, D_MODEL))]
        args += [o_sb, o_dl, g_sb.reshape(1, D_GROUP), g_dl.reshape(1, D_GROUP), w_out]
    in_specs += [vec(D_MODEL), _resident((D_MODEL, D_FF)), _resident((D_MODEL, D_FF)), _resident((D_FF, D_MODEL))]
    args += [gain.reshape(1, D_MODEL), wg, wu, wd]
    if final_gain is not None:
        in_specs.append(vec(D_MODEL))
        args.append(final_gain.reshape(1, D_MODEL))
    out_specs, out_shape = [row(D_MODEL)], [jax.ShapeDtypeStruct((t, D_MODEL), F32)]
    layer, carried = round_later if round_later is not None else (0, ())
    for w in carried:
        _, rows, cols = w.shape
        hold = next(k for k in range(1, n_steps + 1) if n_steps % k == 0 and rows % (n_steps // k * 16) == 0)
        blk = rows // (n_steps // hold)
        in_specs.append(pl.BlockSpec((None, blk, cols), lambda i, hold=hold: (layer, i // hold, 0)))
        out_specs.append(pl.BlockSpec((blk, cols), lambda i, hold=hold: (i // hold, 0)))
        out_shape.append(jax.ShapeDtypeStruct((rows, cols), BF16))
        args.append(w)
    out = pl.pallas_call(
        functools.partial(_ffn_kernel, mixer_out=mixer_out is not None, final_norm=final_gain is not None,
                          n_carried=len(carried)),
        grid=(n_steps,),
        in_specs=in_specs,
        out_specs=out_specs,
        out_shape=out_shape,
        compiler_params=_params("arbitrary" if carried else "parallel"),
        name="ffn_block",
    )(*args)
    return out if carried else out[0]


def _in_proj_kernel(x_ref, gain_ref, w_ref, cos_ref, sin_ref,
                    qsb_ref, ktsb_ref, vsb_ref, qdl_ref, kdl_ref, vdl_ref):
    h = (_rms(x_ref[...]) * gain_ref[...]).astype(BF16)

    def proj(j):
        return jnp.dot(h, w_ref[:, j * D_GROUP:(j + 1) * D_GROUP], preferred_element_type=F32)

    cos = cos_ref[...]
    sin = sin_ref[...]
    lane = lax.broadcasted_iota(jnp.int32, cos.shape, 1)
    low_half = (lane % HEAD_DIM) < HEAD_DIM // 2

    def rope(t):
        outs = []
        for j in range(D_GROUP // LANES):
            tj = t[:, j * LANES:(j + 1) * LANES]
            ahead = pltpu.roll(tj, LANES - HEAD_DIM // 2, 1)
            behind = pltpu.roll(tj, HEAD_DIM // 2, 1)
            outs.append(tj * cos + jnp.where(low_half, ahead, behind) * sin)
        return jnp.concatenate(outs, axis=1)

    qsb_ref[...] = (proj(0) * Q_SCALE).astype(BF16)
    kt = proj(1).T
    for hp in range(D_GROUP // LANES):
        for kb in range(PROJ_TM // SB_KB):
            ktsb_ref[hp, kb] = kt[hp * LANES:(hp + 1) * LANES, kb * SB_KB:(kb + 1) * SB_KB].astype(BF16)
    vsb_ref[...] = proj(2).astype(BF16)
    qdl_ref[...] = (rope(proj(3)) * Q_SCALE).astype(BF16)
    kdl_ref[...] = rope(proj(4)).astype(BF16)
    vdl_ref[...] = proj(5).astype(BF16)


def _in_proj(x, gain, w_in, cos, sin, batch, seq):
    t = x.shape[0]
    n_seq_tiles = seq // PROJ_TM
    row = lambda w: pl.BlockSpec((PROJ_TM, w), lambda b, i: (b * n_seq_tiles + i, 0))
    grp = jax.ShapeDtypeStruct((t, D_GROUP), BF16)
    kt_shape = (batch, D_GROUP // LANES, seq // SB_KB, LANES, SB_KB)
    return pl.pallas_call(
        _in_proj_kernel,
        grid=(batch, n_seq_tiles),
        in_specs=[row(D_MODEL), _resident((1, D_MODEL)), _resident((D_MODEL, D_IN)),
                  pl.BlockSpec((PROJ_TM, LANES), lambda b, i: (i, 0)),
                  pl.BlockSpec((PROJ_TM, LANES), lambda b, i: (i, 0))],
        out_specs=[row(D_GROUP),
                   pl.BlockSpec((None, D_GROUP // LANES, PROJ_TM // SB_KB, LANES, SB_KB),
                                lambda b, i: (b, 0, i, 0, 0)),
                   row(D_GROUP), row(D_GROUP), row(D_GROUP), row(D_GROUP)],
        out_shape=[grp, jax.ShapeDtypeStruct(kt_shape, BF16), grp, grp, grp, grp],
        compiler_params=_params("parallel", "parallel"),
        name="in_proj",
    )(x, gain.reshape(1, D_MODEL), w_in, cos, sin)


def _sb_kernel(q_ref, kt_ref, v_ref, csum_ref, o_ref, qs_ref, acc_ref, run_ref):
    i = pl.program_id(2)
    n_heads = LANES // HEAD_DIM
    lane = lax.broadcasted_iota(jnp.int32, (SB_TQ, LANES), 1)
    for hp in range(SB_PAIRS):
        q = q_ref[:, hp * LANES:(hp + 1) * LANES]
        for hh in range(n_heads):
            in_head = (lane >= hh * HEAD_DIM) & (lane < (hh + 1) * HEAD_DIM)
            qs_ref[hp, hh * SB_TQ:(hh + 1) * SB_TQ, :] = jnp.where(in_head, q, jnp.zeros_like(q))
    run_ref[...] = jnp.zeros_like(run_ref)
    acc_ref[...] = jnp.zeros_like(acc_ref)
    csum_w = csum_ref[...]

    def softplus2(kb, mask):
        out = []
        for hp in range(SB_PAIRS):
            w = jnp.dot(qs_ref[hp], kt_ref[hp, kb], preferred_element_type=F32)
            p = jnp.maximum(w, 0.0) + jnp.log(1.0 + jnp.exp2(-jnp.abs(w))) * INV_LN2
            out.append((w, p if mask is None else jnp.where(mask, p, 0.0)))
        return out

    def neg_cumsums(parts):
        stacked = jnp.concatenate([p for _, p in parts], axis=0).astype(BF16)
        out = jnp.dot(stacked, csum_w, preferred_element_type=F32)
        m = parts[0][1].shape[0]
        return [out[n * m:(n + 1) * m] for n in range(len(parts))]

    def finish(kb, parts, neg_cs, mask):
        for hp, ((w, p), neg_c) in enumerate(zip(parts, neg_cs)):
            run = run_ref[hp]
            a = jnp.exp2((w - p) + neg_c + jnp.concatenate([run] * (SB_KB // LANES), axis=1))
            if mask is not None:
                a = jnp.where(mask, a, 0.0)
            run_ref[hp] = run + jnp.broadcast_to(neg_c[:, 0:1] - p[:, 0:1], run.shape)
            v = v_ref[pl.ds(pl.multiple_of(kb * SB_KB, SB_KB), SB_KB), hp * LANES:(hp + 1) * LANES]
            acc_ref[hp] += jnp.dot(a.astype(BF16), v, preferred_element_type=F32)

    def block(kb, mask):
        parts = softplus2(kb, mask)
        finish(kb, parts, neg_cumsums(parts), mask)

    row = lax.broadcasted_iota(jnp.int32, (n_heads * SB_TQ, SB_KB), 0) % SB_TQ
    col = lax.broadcasted_iota(jnp.int32, (n_heads * SB_TQ, SB_KB), 1)
    strictly_before = col < row
    previous = jnp.maximum(i - 1, 0)
    diag, prev = softplus2(i, strictly_before), softplus2(previous, None)
    neg_cs = neg_cumsums(diag + prev)
    finish(i, diag, neg_cs[:SB_PAIRS], strictly_before)
    run_ref[...] = jnp.where(i > 0, run_ref[...], 2 * SB_DEAD_LOG2)
    finish(previous, prev, neg_cs[SB_PAIRS:], None)

    def alive():
        return (jnp.max(run_ref[...]) > SB_DEAD_LOG2).astype(jnp.int32)

    def older(carry):
        kb, _ = carry
        block(kb, None)
        return kb - 1, alive()

    lax.while_loop(lambda c: (c[0] >= 0) & (c[1] > 0), older, (i - 2, alive()))
    for hp in range(SB_PAIRS):
        o_ref[:, hp * LANES:(hp + 1) * LANES] = jnp.where(lane < HEAD_DIM, acc_ref[hp, :SB_TQ, :], acc_ref[hp, SB_TQ:, :])


def _csum_weights():
    j = jnp.arange(SB_KB)[:, None]
    s = jnp.arange(SB_KB)[None, :]
    return jnp.where(j > s, -1.0, 0.0).astype(BF16)


def _sb_attention(q, kt, v, batch, seq):
    t = q.shape[0]
    n_q = seq // SB_TQ
    n_groups = D_GROUP // (SB_PAIRS * LANES)
    m = (LANES // HEAD_DIM) * SB_TQ
    return pl.pallas_call(
        _sb_kernel,
        grid=(batch, n_groups, n_q),
        in_specs=[pl.BlockSpec((SB_TQ, SB_PAIRS * LANES), lambda b, g, i: (b * n_q + i, g)),
                  pl.BlockSpec((None, SB_PAIRS, seq // SB_KB, LANES, SB_KB), lambda b, g, i: (b, g, 0, 0, 0)),
                  pl.BlockSpec((seq, SB_PAIRS * LANES), lambda b, g, i: (b, g)),
                  _resident((SB_KB, SB_KB))],
        out_specs=pl.BlockSpec((SB_TQ, SB_PAIRS * LANES), lambda b, g, i: (b * n_q + i, g)),
        out_shape=jax.ShapeDtypeStruct((t, D_GROUP), F32),
        scratch_shapes=[pltpu.VMEM((SB_PAIRS, m, LANES), BF16),
                        pltpu.VMEM((SB_PAIRS, m, LANES), F32),
                        pltpu.VMEM((SB_PAIRS, m, LANES), F32)],
        compiler_params=_params("parallel", "parallel", "arbitrary"),
        name="sb_attention",
    )(q, kt, v, _csum_weights())


def _dilated_kernel(q_ref, kp_ref, kc_ref, vp_ref, vc_ref, o_ref,
                    qn_ref, kn_ref, vn_ref, q4_ref, k4_ref, v4_ref,
                    on4_ref, mn4_ref, ln4_ref, on16_ref, mn16_ref, ln16_ref):
    st = pl.program_id(2)
    n_heads = LANES // HEAD_DIM
    lane = lax.broadcasted_iota(jnp.int32, (DL_TQ, LANES), 1)
    row = lax.broadcasted_iota(jnp.int32, (DL_TQ, 2 * DL_TQ), 0)
    col = lax.broadcasted_iota(jnp.int32, (DL_TQ, 2 * DL_TQ), 1)
    band = (col >= row) & (col <= row + DL_TQ)
    band_edge = band & (col >= jnp.where(st == 0, DL_TQ, 0))
    head_lanes = [(lane >= hh * HEAD_DIM) & (lane < (hh + 1) * HEAD_DIM) for hh in range(n_heads)]

    qn_ref[...] = q_ref[...].astype(F32)
    kn_ref[:DL_SPAN, :] = kp_ref[...].astype(F32)
    kn_ref[DL_SPAN:, :] = kc_ref[...].astype(F32)
    vn_ref[:DL_SPAN, :] = vp_ref[...].astype(F32)
    vn_ref[DL_SPAN:, :] = vc_ref[...].astype(F32)

    def attend(q, k, v, valid):
        qs = jnp.concatenate([jnp.where(head_lanes[hh], q, jnp.zeros_like(q)) for hh in range(n_heads)], axis=0)
        z = lax.dot_general(qs, k, (((1,), (1,)), ((), ())), preferred_element_type=F32)
        z = jnp.where(jnp.concatenate([valid] * n_heads, axis=0), z, -jnp.inf)
        m = jnp.max(z, axis=-1, keepdims=True)
        p = jnp.exp2(z - m)
        den = jnp.sum(p, axis=-1, keepdims=True)
        num = jnp.dot(p.astype(BF16), v, preferred_element_type=F32)
        m, den = jnp.broadcast_to(m, num.shape), jnp.broadcast_to(den, num.shape)
        return tuple(jnp.where(lane < HEAD_DIM, t[:DL_TQ], t[DL_TQ:]) for t in (num, m, den))

    for src, dst in ((qn_ref, q4_ref), (kn_ref, k4_ref), (vn_ref, v4_ref)):
        per_class = src.shape[0] // 4
        for c in range(4):
            dst[c * per_class:(c + 1) * per_class, :] = src[pl.ds(c, per_class, stride=4), :]

    def strided_pattern(d, on_ref, mn_ref, ln_ref):
        sub = d // 4
        q_per_class, k_per_class = DL_SPAN // 4, 2 * DL_SPAN // 4
        for c in range(d):
            c4, r = c % 4, c // 4
            for j in range(DL_SPAN // (d * DL_TQ)):
                q0 = c + d * DL_TQ * j
                qi = r + sub * DL_TQ * j
                ki = k_per_class // 2 + qi - sub * DL_TQ
                q = q4_ref[pl.ds(c4 * q_per_class + qi, DL_TQ, stride=sub), :].astype(BF16)
                k = k4_ref[pl.ds(c4 * k_per_class + ki, 2 * DL_TQ, stride=sub), :].astype(BF16)
                v = v4_ref[pl.ds(c4 * k_per_class + ki, 2 * DL_TQ, stride=sub), :].astype(BF16)
                num, m, den = attend(q, k, v, band_edge if j == 0 else band)
                on_ref[pl.ds(q0, DL_TQ, stride=d), :] = num
                mn_ref[pl.ds(q0, DL_TQ, stride=d), :] = m
                ln_ref[pl.ds(q0, DL_TQ, stride=d), :] = den

    strided_pattern(16, on16_ref, mn16_ref, ln16_ref)
    strided_pattern(4, on4_ref, mn4_ref, ln4_ref)

    def mix(part1, r0):
        rows = pl.ds(r0, DL_TQ)
        parts = (part1, (on4_ref[rows, :], mn4_ref[rows, :], ln4_ref[rows, :]),
                 (on16_ref[rows, :], mn16_ref[rows, :], ln16_ref[rows, :]))
        top = functools.reduce(jnp.maximum, [m for _, m, _ in parts])
        scale = [jnp.exp2(m - top) for _, m, _ in parts]
        num = sum(s * n for s, (n, _, _) in zip(scale, parts))
        den = sum(s * d for s, (_, _, d) in zip(scale, parts))
        return num / den

    k_edge = jnp.concatenate([kp_ref[DL_SPAN - DL_TQ:, :], kc_ref[:DL_TQ, :]], axis=0)
    v_edge = jnp.concatenate([vp_ref[DL_SPAN - DL_TQ:, :], vc_ref[:DL_TQ, :]], axis=0)
    o_ref[:DL_TQ, :] = mix(attend(q_ref[:DL_TQ, :], k_edge, v_edge, band_edge), 0)

    for j in range(1, DL_SPAN // DL_TQ):
        q0, k0 = j * DL_TQ, (j - 1) * DL_TQ
        part = attend(q_ref[q0:q0 + DL_TQ, :], kc_ref[k0:k0 + 2 * DL_TQ, :], vc_ref[k0:k0 + 2 * DL_TQ, :], band)
        o_ref[q0:q0 + DL_TQ, :] = mix(part, q0)


def _dilated_attention(q, k, v, batch, seq):
    t = q.shape[0]
    assert seq % DL_SPAN == 0
    n_span = seq // DL_SPAN
    cur = pl.BlockSpec((DL_SPAN, LANES), lambda b, hp, st: (b * n_span + st, hp))
    prev = pl.BlockSpec((DL_SPAN, LANES), lambda b, hp, st: (b * n_span + jnp.maximum(st - 1, 0), hp))
    nat = pltpu.VMEM((DL_SPAN, LANES), F32)
    both = pltpu.VMEM((2 * DL_SPAN, LANES), F32)
    return pl.pallas_call(
        _dilated_kernel,
        grid=(batch, D_GROUP // LANES, n_span),
        in_specs=[cur, prev, cur, prev, cur],
        out_specs=cur,
        out_shape=jax.ShapeDtypeStruct((t, D_GROUP), F32),
        scratch_shapes=[nat, both, both, nat, both, both, nat, nat, nat, nat, nat, nat],
        compiler_params=_params("parallel", "parallel", "arbitrary"),
        name="dilated_mixture",
    )(q, k, k, v, v)


def _cast_kernel(*refs):
    n = len(refs) // 2
    for src, dst in zip(refs[:n], refs[n:]):
        dst[...] = src[...].astype(dst.dtype)


def _layer_weights_bf16(layer, *stacked):
    in_specs, out_specs, out_shape = [], [], []
    for w in stacked:
        _, rows, cols = w.shape
        assert rows % (CAST_STEPS * 16) == 0
        in_specs.append(pl.BlockSpec((None, rows // CAST_STEPS, cols), lambda i: (layer, i, 0)))
        out_specs.append(pl.BlockSpec((rows // CAST_STEPS, cols), lambda i: (i, 0)))
        out_shape.append(jax.ShapeDtypeStruct((rows, cols), BF16))
    return pl.pallas_call(
        _cast_kernel,
        grid=(CAST_STEPS,),
        in_specs=in_specs,
        out_specs=out_specs,
        out_shape=out_shape,
        compiler_params=_params("parallel"),
        name="weights_bf16",
    )(*stacked)


def _rope_tables(seq):
    half = HEAD_DIM // 2
    inv_freq = ROPE_THETA ** (-np.arange(half, dtype=np.float64) / half)
    ang = np.arange(seq, dtype=np.float64)[:, None] * inv_freq[None, :]
    cos, sin = np.cos(ang), np.sin(ang)
    reps = LANES // HEAD_DIM
    cos_t = np.tile(np.concatenate([cos, cos], axis=1), (1, reps))
    sin_t = np.tile(np.concatenate([-sin, sin], axis=1), (1, reps))
    return jnp.asarray(cos_t, F32), jnp.asarray(sin_t, F32)


def kernel(x, ffn1_norm, ffn1_w_gate, ffn1_w_up, ffn1_w_down, mix_norm, w_in, sb_out_norm, dil_out_norm, w_out, ffn2_norm, ffn2_w_gate, ffn2_w_up, ffn2_w_down, final_norm):
    batch, seq, _ = x.shape
    depth = ffn1_norm.shape[0]
    cos, sin = _rope_tables(seq)
    h = x.reshape(batch * seq, D_MODEL)
    for layer in range(depth):
        g1, u1, d1 = _layer_weights_bf16(layer, ffn1_w_gate, ffn1_w_up, ffn1_w_down)
        h, w_qkv, w_o, g2, u2, d2 = _ffn_block(
            h, ffn1_norm[layer], g1, u1, d1,
            round_later=(layer, (w_in, w_out, ffn2_w_gate, ffn2_w_up, ffn2_w_down)))
        q_sb, kt_sb, v_sb, q_dl, k_dl, v_dl = _in_proj(h, mix_norm[layer], w_qkv, cos, sin, batch, seq)
        o_sb = _sb_attention(q_sb, kt_sb, v_sb, batch, seq)
        o_dl = _dilated_attention(q_dl, k_dl, v_dl, batch, seq)
        h = _ffn_block(h, ffn2_norm[layer], g2, u2, d2,
                       mixer_out=(o_sb, o_dl, sb_out_norm[layer], dil_out_norm[layer], w_o),
                       final_gain=final_norm if layer == depth - 1 else None)
    return h.reshape(batch, seq, D_MODEL)
```

```python
import functools
import math

import jax
import jax.numpy as jnp
import numpy as np
from jax import lax
from jax.experimental import pallas as pl
from jax.experimental.pallas import tpu as pltpu

D_MODEL = 1024
HEAD_DIM = 64
D_GROUP = 512
D_IN = 6 * D_GROUP
D_FF = 2816
DILATED_PATTERNS = ((128, 1), (512, 4), (2048, 16))
ROPE_THETA = 10000.0
RMS_EPS = 1e-6
HALF_STEP = 0.5

LANES = 128
LOG2E = math.log2(math.e)
INV_LN2 = 1.0 / math.log(2.0)
Q_SCALE = HEAD_DIM ** -0.5 * LOG2E

TM = 512
PROJ_TM = 1024
FF_CHUNK = 256
SB_TQ = 256
SB_PAIRS = 4
SB_KB = 256
SB_DEAD_LOG2 = -160.0
assert SB_TQ == SB_KB
DL_TQ = 128
DL_SPAN = DL_TQ * max(d for _, d in DILATED_PATTERNS)
assert [d for _, d in DILATED_PATTERNS] == [1, 4, 16] and all(w // d == DL_TQ for w, d in DILATED_PATTERNS)
CAST_STEPS = 8
VMEM_LIMIT = 48 * 1024 * 1024

F32 = jnp.float32
BF16 = jnp.bfloat16


def _rms(x):
    return x * lax.rsqrt(jnp.mean(x * x, axis=-1, keepdims=True) + RMS_EPS)


def _params(*sem):
    return pltpu.CompilerParams(dimension_semantics=sem, vmem_limit_bytes=VMEM_LIMIT)


def _resident(shape):
    return pl.BlockSpec(shape, lambda *_: (0,) * len(shape), pipeline_mode=pl.Buffered(1))


def _ffn_kernel(*refs, mixer_out, final_norm, n_carried):
    refs = list(refs)
    carried_out = [refs.pop() for _ in range(n_carried)][::-1]
    o_ref = refs.pop()
    carried_in = [refs.pop() for _ in range(n_carried)][::-1]
    for src, dst in zip(carried_in, carried_out):
        dst[...] = src[...].astype(dst.dtype)
    x = refs.pop(0)[...]
    if mixer_out:
        osb_ref, odl_ref, gsb_ref, gdl_ref, wout_ref = refs[:5]
        del refs[:5]
        sb = (_rms(osb_ref[...]) * gsb_ref[...]).astype(BF16)
        dl = (_rms(odl_ref[...]) * gdl_ref[...]).astype(BF16)
        x = x + jnp.dot(sb, wout_ref[:D_GROUP, :], preferred_element_type=F32)
        x = x + jnp.dot(dl, wout_ref[D_GROUP:, :], preferred_element_type=F32)
    gain_ref, wg_ref, wu_ref, wd_ref = refs[:4]
    h = (_rms(x) * gain_ref[...]).astype(BF16)
    acc = jnp.zeros(x.shape, F32)
    for c in range(0, D_FF, FF_CHUNK):
        g = jnp.dot(h, wg_ref[:, c:c + FF_CHUNK], preferred_element_type=F32)
        u = jnp.dot(h, wu_ref[:, c:c + FF_CHUNK], preferred_element_type=F32)
        a = (g * jax.nn.sigmoid(g) * u).astype(BF16)
        acc = acc + jnp.dot(a, wd_ref[c:c + FF_CHUNK, :], preferred_element_type=F32)
    y = x + HALF_STEP * acc
    if final_norm:
        y = _rms(y) * refs[4][...]
    o_ref[...] = y


def _ffn_block(x, gain, wg, wu, wd, mixer_out=None, final_gain=None, round_later=None):
    t = x.shape[0]
    n_steps = t // TM
    row = lambda w: pl.BlockSpec((TM, w), lambda i: (i, 0))
    vec = lambda w: _resident((1, w))
    in_specs, args = [row(D_MODEL)], [x]
    if mixer_out is not None:
        o_sb, o_dl, g_sb, g_dl, w_out = mixer_out
        in_specs += [row(D_GROUP), row(D_GROUP), vec(D_GROUP), vec(D_GROUP), _resident((D_MODEL, D_MODEL))]
        args += [o_sb, o_dl, g_sb.reshape(1, D_GROUP), g_dl.reshape(1, D_GROUP), w_out]
    in_specs += [vec(D_MODEL), _resident((D_MODEL, D_FF)), _resident((D_MODEL, D_FF)), _resident((D_FF, D_MODEL))]
    args += [gain.reshape(1, D_MODEL), wg, wu, wd]
    if final_gain is not None:
        in_specs.append(vec(D_MODEL))
        args.append(final_gain.reshape(1, D_MODEL))
    out_specs, out_shape = [row(D_MODEL)], [jax.ShapeDtypeStruct((t, D_MODEL), F32)]
    layer, carried = round_later if round_later is not None else (0, ())
    for w in carried:
        _, rows, cols = w.shape
        hold = next(k for k in range(1, n_steps + 1) if n_steps % k == 0 and rows % (n_steps // k * 16) == 0)
        blk = rows // (n_steps // hold)
        in_specs.append(pl.BlockSpec((None, blk, cols), lambda i, hold=hold: (layer, i // hold, 0)))
        out_specs.append(pl.BlockSpec((blk, cols), lambda i, hold=hold: (i // hold, 0)))
        out_shape.append(jax.ShapeDtypeStruct((rows, cols), BF16))
        args.append(w)
    out = pl.pallas_call(
        functools.partial(_ffn_kernel, mixer_out=mixer_out is not None, final_norm=final_gain is not None,
                          n_carried=len(carried)),
        grid=(n_steps,),
        in_specs=in_specs,
        out_specs=out_specs,
        out_shape=out_shape,
        compiler_params=_params("arbitrary" if carried else "parallel"),
        name="ffn_block",
    )(*args)
    return out if carried else out[0]


def _in_proj_kernel(x_ref, gain_ref, w_ref, cos_ref, sin_ref,
                    qsb_ref, ktsb_ref, vsb_ref, qdl_ref, kdl_ref, vdl_ref):
    h = (_rms(x_ref[...]) * gain_ref[...]).astype(BF16)

    def proj(j):
        return jnp.dot(h, w_ref[:, j * D_GROUP:(j + 1) * D_GROUP], preferred_element_type=F32)

    cos = cos_ref[...]
    sin = sin_ref[...]
    lane = lax.broadcasted_iota(jnp.int32, cos.shape, 1)
    low_half = (lane % HEAD_DIM) < HEAD_DIM // 2

    def rope(t):
        outs = []
        for j in range(D_GROUP // LANES):
            tj = t[:, j * LANES:(j + 1) * LANES]
            ahead = pltpu.roll(tj, LANES - HEAD_DIM // 2, 1)
            behind = pltpu.roll(tj, HEAD_DIM // 2, 1)
            outs.append(tj * cos + jnp.where(low_half, ahead, behind) * sin)
        return jnp.concatenate(outs, axis=1)

    qsb_ref[...] = (proj(0) * Q_SCALE).astype(BF16)
    kt = proj(1).T
    for hp in range(D_GROUP // LANES):
        for kb in range(PROJ_TM // SB_KB):
            ktsb_ref[hp, kb] = kt[hp * LANES:(hp + 1) * LANES, kb * SB_KB:(kb + 1) * SB_KB].astype(BF16)
    vsb_ref[...] = proj(2).astype(BF16)
    qdl_ref[...] = (rope(proj(3)) * Q_SCALE).astype(BF16)
    kdl_ref[...] = rope(proj(4)).astype(BF16)
    vdl_ref[...] = proj(5).astype(BF16)


def _in_proj(x, gain, w_in, cos, sin, batch, seq):
    t = x.shape[0]
    n_seq_tiles = seq // PROJ_TM
    row = lambda w: pl.BlockSpec((PROJ_TM, w), lambda b, i: (b * n_seq_tiles + i, 0))
    grp = jax.ShapeDtypeStruct((t, D_GROUP), BF16)
    kt_shape = (batch, D_GROUP // LANES, seq // SB_KB, LANES, SB_KB)
    return pl.pallas_call(
        _in_proj_kernel,
        grid=(batch, n_seq_tiles),
        in_specs=[row(D_MODEL), _resident((1, D_MODEL)), _resident((D_MODEL, D_IN)),
                  pl.BlockSpec((PROJ_TM, LANES), lambda b, i: (i, 0)),
                  pl.BlockSpec((PROJ_TM, LANES), lambda b, i: (i, 0))],
        out_specs=[row(D_GROUP),
                   pl.BlockSpec((None, D_GROUP // LANES, PROJ_TM // SB_KB, LANES, SB_KB),
                                lambda b, i: (b, 0, i, 0, 0)),
                   row(D_GROUP), row(D_GROUP), row(D_GROUP), row(D_GROUP)],
        out_shape=[grp, jax.ShapeDtypeStruct(kt_shape, BF16), grp, grp, grp, grp],
        compiler_params=_params("parallel", "parallel"),
        name="in_proj",
    )(x, gain.reshape(1, D_MODEL), w_in, cos, sin)


def _sb_kernel(q_ref, kt_ref, v_ref, csum_ref, o_ref, qs_ref, acc_ref, run_ref):
    i = pl.program_id(2)
    n_heads = LANES // HEAD_DIM
    lane = lax.broadcasted_iota(jnp.int32, (SB_TQ, LANES), 1)
    for hp in range(SB_PAIRS):
        q = q_ref[:, hp * LANES:(hp + 1) * LANES]
        for hh in range(n_heads):
            in_head = (lane >= hh * HEAD_DIM) & (lane < (hh + 1) * HEAD_DIM)
            qs_ref[hp, hh * SB_TQ:(hh + 1) * SB_TQ, :] = jnp.where(in_head, q, jnp.zeros_like(q))
    run_ref[...] = jnp.zeros_like(run_ref)
    acc_ref[...] = jnp.zeros_like(acc_ref)
    csum_w = csum_ref[...]

    def softplus2(kb, mask):
        out = []
        for hp in range(SB_PAIRS):
            w = jnp.dot(qs_ref[hp], kt_ref[hp, kb], preferred_element_type=F32)
            p = jnp.maximum(w, 0.0) + jnp.log(1.0 + jnp.exp2(-jnp.abs(w))) * INV_LN2
            out.append((w, p if mask is None else jnp.where(mask, p, 0.0)))
        return out

    def neg_cumsums(parts):
        stacked = jnp.concatenate([p for _, p in parts], axis=0).astype(BF16)
        out = jnp.dot(stacked, csum_w, preferred_element_type=F32)
        m = parts[0][1].shape[0]
        return [out[n * m:(n + 1) * m] for n in range(len(parts))]

    def finish(kb, parts, neg_cs, mask):
        for hp, ((w, p), neg_c) in enumerate(zip(parts, neg_cs)):
            run = run_ref[hp]
            a = jnp.exp2((w - p) + neg_c + jnp.concatenate([run] * (SB_KB // LANES), axis=1))
            if mask is not None:
                a = jnp.where(mask, a, 0.0)
            run_ref[hp] = run + jnp.broadcast_to(neg_c[:, 0:1] - p[:, 0:1], run.shape)
            v = v_ref[pl.ds(pl.multiple_of(kb * SB_KB, SB_KB), SB_KB), hp * LANES:(hp + 1) * LANES]
            acc_ref[hp] += jnp.dot(a.astype(BF16), v, preferred_element_type=F32)

    def block(kb, mask):
        parts = softplus2(kb, mask)
        finish(kb, parts, neg_cumsums(parts), mask)

    row = lax.broadcasted_iota(jnp.int32, (n_heads * SB_TQ, SB_KB), 0) % SB_TQ
    col = lax.broadcasted_iota(jnp.int32, (n_heads * SB_TQ, SB_KB), 1)
    strictly_before = col < row
    previous = jnp.maximum(i - 1, 0)
    diag, prev = softplus2(i, strictly_before), softplus2(previous, None)
    neg_cs = neg_cumsums(diag + prev)
    finish(i, diag, neg_cs[:SB_PAIRS], strictly_before)
    run_ref[...] = jnp.where(i > 0, run_ref[...], 2 * SB_DEAD_LOG2)
    finish(previous, prev, neg_cs[SB_PAIRS:], None)

    def alive():
        return (jnp.max(run_ref[...]) > SB_DEAD_LOG2).astype(jnp.int32)

    def older(carry):
        kb, _ = carry
        block(kb, None)
        return kb - 1, alive()

    lax.while_loop(lambda c: (c[0] >= 0) & (c[1] > 0), older, (i - 2, alive()))
    for hp in range(SB_PAIRS):
        o_ref[:, hp * LANES:(hp + 1) * LANES] = jnp.where(lane < HEAD_DIM, acc_ref[hp, :SB_TQ, :], acc_ref[hp, SB_TQ:, :])


def _csum_weights():
    j = jnp.arange(SB_KB)[:, None]
    s = jnp.arange(SB_KB)[None, :]
    return jnp.where(j > s, -1.0, 0.0).astype(BF16)


def _sb_attention(q, kt, v, batch, seq):
    t = q.shape[0]
    n_q = seq // SB_TQ
    n_groups = D_GROUP // (SB_PAIRS * LANES)
    m = (LANES // HEAD_DIM) * SB_TQ
    return pl.pallas_call(
        _sb_kernel,
        grid=(batch, n_groups, n_q),
        in_specs=[pl.BlockSpec((SB_TQ, SB_PAIRS * LANES), lambda b, g, i: (b * n_q + i, g)),
                  pl.BlockSpec((None, SB_PAIRS, seq // SB_KB, LANES, SB_KB), lambda b, g, i: (b, g, 0, 0, 0)),
                  pl.BlockSpec((seq, SB_PAIRS * LANES), lambda b, g, i: (b, g)),
                  _resident((SB_KB, SB_KB))],
        out_specs=pl.BlockSpec((SB_TQ, SB_PAIRS * LANES), lambda b, g, i: (b * n_q + i, g)),
        out_shape=jax.ShapeDtypeStruct((t, D_GROUP), F32),
        scratch_shapes=[pltpu.VMEM((SB_PAIRS, m, LANES), BF16),
                        pltpu.VMEM((SB_PAIRS, m, LANES), F32),
                        pltpu.VMEM((SB_PAIRS, m, LANES), F32)],
        compiler_params=_params("parallel", "parallel", "arbitrary"),
        name="sb_attention",
    )(q, kt, v, _csum_weights())


def _dilated_kernel(q_ref, kp_ref, kc_ref, vp_ref, vc_ref, o_ref,
                    qn_ref, kn_ref, vn_ref, q4_ref, k4_ref, v4_ref,
                    on4_ref, mn4_ref, ln4_ref, on16_ref, mn16_ref, ln16_ref, tn_ref, tm_ref, tl_ref):
    st = pl.program_id(2)
    n_heads = LANES // HEAD_DIM
    lane = lax.broadcasted_iota(jnp.int32, (DL_TQ, LANES), 1)
    row = lax.broadcasted_iota(jnp.int32, (DL_TQ, 2 * DL_TQ), 0)
    col = lax.broadcasted_iota(jnp.int32, (DL_TQ, 2 * DL_TQ), 1)
    band = (col >= row) & (col <= row + DL_TQ)
    band_edge = band & (col >= jnp.where(st == 0, DL_TQ, 0))
    head_lanes = [(lane >= hh * HEAD_DIM) & (lane < (hh + 1) * HEAD_DIM) for hh in range(n_heads)]

    qn_ref[...] = q_ref[...].astype(F32)
    kn_ref[:DL_SPAN, :] = kp_ref[...].astype(F32)
    kn_ref[DL_SPAN:, :] = kc_ref[...].astype(F32)
    vn_ref[:DL_SPAN, :] = vp_ref[...].astype(F32)
    vn_ref[DL_SPAN:, :] = vc_ref[...].astype(F32)

    def attend(q, k, v, valid):
        qs = jnp.concatenate([jnp.where(head_lanes[hh], q, jnp.zeros_like(q)) for hh in range(n_heads)], axis=0)
        z = lax.dot_general(qs, k, (((1,), (1,)), ((), ())), preferred_element_type=F32)
        z = jnp.where(jnp.concatenate([valid] * n_heads, axis=0), z, -jnp.inf)
        m = jnp.max(z, axis=-1, keepdims=True)
        p = jnp.exp2(z - m)
        den = jnp.sum(p, axis=-1, keepdims=True)
        num = jnp.dot(p.astype(BF16), v, preferred_element_type=F32)
        m, den = jnp.broadcast_to(m, num.shape), jnp.broadcast_to(den, num.shape)
        return tuple(jnp.where(lane < HEAD_DIM, t[:DL_TQ], t[DL_TQ:]) for t in (num, m, den))

    for src, dst in ((qn_ref, q4_ref), (kn_ref, k4_ref), (vn_ref, v4_ref)):
        per_class = src.shape[0] // 4
        for c in range(4):
            dst[c * per_class:(c + 1) * per_class, :] = src[pl.ds(c, per_class, stride=4), :]

    def strided_pattern(d, on_ref, mn_ref, ln_ref):
        sub = d // 4
        q_per_class, k_per_class = DL_SPAN // 4, 2 * DL_SPAN // 4
        for c in range(d):
            c4, r = c % 4, c // 4
            for j in range(DL_SPAN // (d * DL_TQ)):
                q0 = c + d * DL_TQ * j
                qi = r + sub * DL_TQ * j
                ki = k_per_class // 2 + qi - sub * DL_TQ
                q = q4_ref[pl.ds(c4 * q_per_class + qi, DL_TQ, stride=sub), :].astype(BF16)
                k = k4_ref[pl.ds(c4 * k_per_class + ki, 2 * DL_TQ, stride=sub), :].astype(BF16)
                v = v4_ref[pl.ds(c4 * k_per_class + ki, 2 * DL_TQ, stride=sub), :].astype(BF16)
                num, m, den = attend(q, k, v, band_edge if j == 0 else band)
                if sub == 1:
                    rows = pl.ds(q0, DL_TQ, stride=d)
                    on_ref[rows, :], mn_ref[rows, :], ln_ref[rows, :] = num, m, den
                else:
                    rows = pl.ds(c4 * q_per_class + qi, DL_TQ, stride=sub)
                    tn_ref[rows, :], tm_ref[rows, :], tl_ref[rows, :] = num, m, den
        if sub > 1:
            for src, dst in ((tn_ref, on_ref), (tm_ref, mn_ref), (tl_ref, ln_ref)):
                for c in range(4):
                    dst[pl.ds(c, q_per_class, stride=4), :] = src[c * q_per_class:(c + 1) * q_per_class, :]

    strided_pattern(16, on16_ref, mn16_ref, ln16_ref)
    strided_pattern(4, on4_ref, mn4_ref, ln4_ref)

    def mix(part1, r0):
        rows = pl.ds(r0, DL_TQ)
        parts = (part1, (on4_ref[rows, :], mn4_ref[rows, :], ln4_ref[rows, :]),
                 (on16_ref[rows, :], mn16_ref[rows, :], ln16_ref[rows, :]))
        top = functools.reduce(jnp.maximum, [m for _, m, _ in parts])
        scale = [jnp.exp2(m - top) for _, m, _ in parts]
        num = sum(s * n for s, (n, _, _) in zip(scale, parts))
        den = sum(s * d for s, (_, _, d) in zip(scale, parts))
        return num / den

    k_edge = jnp.concatenate([kp_ref[DL_SPAN - DL_TQ:, :], kc_ref[:DL_TQ, :]], axis=0)
    v_edge = jnp.concatenate([vp_ref[DL_SPAN - DL_TQ:, :], vc_ref[:DL_TQ, :]], axis=0)
    o_ref[:DL_TQ, :] = mix(attend(q_ref[:DL_TQ, :], k_edge, v_edge, band_edge), 0)

    for j in range(1, DL_SPAN // DL_TQ):
        q0, k0 = j * DL_TQ, (j - 1) * DL_TQ
        part = attend(q_ref[q0:q0 + DL_TQ, :], kc_ref[k0:k0 + 2 * DL_TQ, :], vc_ref[k0:k0 + 2 * DL_TQ, :], band)
        o_ref[q0:q0 + DL_TQ, :] = mix(part, q0)


def _dilated_attention(q, k, v, batch, seq):
    t = q.shape[0]
    assert seq % DL_SPAN == 0
    n_span = seq // DL_SPAN
    cur = pl.BlockSpec((DL_SPAN, LANES), lambda b, hp, st: (b * n_span + st, hp))
    prev = pl.BlockSpec((DL_SPAN, LANES), lambda b, hp, st: (b * n_span + jnp.maximum(st - 1, 0), hp))
    nat = pltpu.VMEM((DL_SPAN, LANES), F32)
    both = pltpu.VMEM((2 * DL_SPAN, LANES), F32)
    return pl.pallas_call(
        _dilated_kernel,
        grid=(batch, D_GROUP // LANES, n_span),
        in_specs=[cur, prev, cur, prev, cur],
        out_specs=cur,
        out_shape=jax.ShapeDtypeStruct((t, D_GROUP), F32),
        scratch_shapes=[nat, both, both, nat, both, both, nat, nat, nat, nat, nat, nat, nat, nat, nat],
        compiler_params=_params("parallel", "parallel", "arbitrary"),
        name="dilated_mixture",
    )(q, k, k, v, v)


def _cast_kernel(*refs):
    n = len(refs) // 2
    for src, dst in zip(refs[:n], refs[n:]):
        dst[...] = src[...].astype(dst.dtype)


def _layer_weights_bf16(layer, *stacked):
    in_specs, out_specs, out_shape = [], [], []
    for w in stacked:
        _, rows, cols = w.shape
        assert rows % (CAST_STEPS * 16) == 0
        in_specs.append(pl.BlockSpec((None, rows // CAST_STEPS, cols), lambda i: (layer, i, 0)))
        out_specs.append(pl.BlockSpec((rows // CAST_STEPS, cols), lambda i: (i, 0)))
        out_shape.append(jax.ShapeDtypeStruct((rows, cols), BF16))
    return pl.pallas_call(
        _cast_kernel,
        grid=(CAST_STEPS,),
        in_specs=in_specs,
        out_specs=out_specs,
        out_shape=out_shape,
        compiler_params=_params("parallel"),
        name="weights_bf16",
    )(*stacked)


def _rope_tables(seq):
    half = HEAD_DIM // 2
    inv_freq = ROPE_THETA ** (-np.arange(half, dtype=np.float64) / half)
    ang = np.arange(seq, dtype=np.float64)[:, None] * inv_freq[None, :]
    cos, sin = np.cos(ang), np.sin(ang)
    reps = LANES // HEAD_DIM
    cos_t = np.tile(np.concatenate([cos, cos], axis=1), (1, reps))
    sin_t = np.tile(np.concatenate([-sin, sin], axis=1), (1, reps))
    return jnp.asarray(cos_t, F32), jnp.asarray(sin_t, F32)


def kernel(x, ffn1_norm, ffn1_w_gate, ffn1_w_up, ffn1_w_down, mix_norm, w_in, sb_out_norm, dil_out_norm, w_out, ffn2_norm, ffn2_w_gate, ffn2_w_up, ffn2_w_down, final_norm):
    batch, seq, _ = x.shape
    depth = ffn1_norm.shape[0]
    cos, sin = _rope_tables(seq)
    h = x.reshape(batch * seq, D_MODEL)
    for layer in range(depth):
        g1, u1, d1 = _layer_weights_bf16(layer, ffn1_w_gate, ffn1_w_up, ffn1_w_down)
        h, w_qkv, w_o, g2, u2, d2 = _ffn_block(
            h, ffn1_norm[layer], g1, u1, d1,
            round_later=(layer, (w_in, w_out, ffn2_w_gate, ffn2_w_up, ffn2_w_down)))
        q_sb, kt_sb, v_sb, q_dl, k_dl, v_dl = _in_proj(h, mix_norm[layer], w_qkv, cos, sin, batch, seq)
        o_sb = _sb_attention(q_sb, kt_sb, v_sb, batch, seq)
        o_dl = _dilated_attention(q_dl, k_dl, v_dl, batch, seq)
        h = _ffn_block(h, ffn2_norm[layer], g2, u2, d2,
                       mixer_out=(o_sb, o_dl, sb_out_norm[layer], dil_out_norm[layer], w_o),
                       final_gain=final_norm if layer == depth - 1 else None)
    return h.reshape(batch, seq, D_MODEL)
```

```python
import functools
import math

import jax
import jax.numpy as jnp
import numpy as np
from jax import lax
from jax.experimental import pallas as pl
from jax.experimental.pallas import tpu as pltpu

D_MODEL = 1024
HEAD_DIM = 64
D_GROUP = 512
D_IN = 6 * D_GROUP
D_FF = 2816
DILATED_PATTERNS = ((128, 1), (512, 4), (2048, 16))
ROPE_THETA = 10000.0
RMS_EPS = 1e-6
HALF_STEP = 0.5

LANES = 128
LOG2E = math.log2(math.e)
INV_LN2 = 1.0 / math.log(2.0)
Q_SCALE = HEAD_DIM ** -0.5 * LOG2E

TM = 1024
PROJ_TM = 1024
FF_CHUNK = 256
SB_TQ = 256
SB_PAIRS = 4
SB_KB = 256
SB_DEAD_LOG2 = -160.0
assert SB_TQ == SB_KB
DL_TQ = 128
DL_SPAN = DL_TQ * max(d for _, d in DILATED_PATTERNS)
assert [d for _, d in DILATED_PATTERNS] == [1, 4, 16] and all(w // d == DL_TQ for w, d in DILATED_PATTERNS)
CAST_STEPS = 8
VMEM_LIMIT = 60 * 1024 * 1024

F32 = jnp.float32
BF16 = jnp.bfloat16


def _rms(x):
    return x * lax.rsqrt(jnp.mean(x * x, axis=-1, keepdims=True) + RMS_EPS)


def _params(*sem):
    return pltpu.CompilerParams(dimension_semantics=sem, vmem_limit_bytes=VMEM_LIMIT)


def _resident(shape):
    return pl.BlockSpec(shape, lambda *_: (0,) * len(shape), pipeline_mode=pl.Buffered(1))


def _ffn_kernel(*refs, mixer_out, final_norm, n_carried):
    refs = list(refs)
    carried_out = [refs.pop() for _ in range(n_carried)][::-1]
    o_ref = refs.pop()
    carried_in = [refs.pop() for _ in range(n_carried)][::-1]
    for src, dst in zip(carried_in, carried_out):
        dst[...] = src[...].astype(dst.dtype)
    x = refs.pop(0)[...]
    if mixer_out:
        osb_ref, odl_ref, gsb_ref, gdl_ref, wout_ref = refs[:5]
        del refs[:5]
        sb = (_rms(osb_ref[...]) * gsb_ref[...]).astype(BF16)
        dl = (_rms(odl_ref[...]) * gdl_ref[...]).astype(BF16)
        x = x + jnp.dot(sb, wout_ref[:D_GROUP, :], preferred_element_type=F32)
        x = x + jnp.dot(dl, wout_ref[D_GROUP:, :], preferred_element_type=F32)
    gain_ref, wg_ref, wu_ref, wd_ref = refs[:4]
    h = (_rms(x) * gain_ref[...]).astype(BF16)
    acc = jnp.zeros(x.shape, F32)
    for c in range(0, D_FF, FF_CHUNK):
        g = jnp.dot(h, wg_ref[:, c:c + FF_CHUNK], preferred_element_type=F32)
        u = jnp.dot(h, wu_ref[:, c:c + FF_CHUNK], preferred_element_type=F32)
        a = (g * jax.nn.sigmoid(g) * u).astype(BF16)
        acc = acc + jnp.dot(a, wd_ref[c:c + FF_CHUNK, :], preferred_element_type=F32)
    y = x + HALF_STEP * acc
    if final_norm:
        y = _rms(y) * refs[4][...]
    o_ref[...] = y


def _ffn_block(x, gain, wg, wu, wd, mixer_out=None, final_gain=None, round_later=None):
    t = x.shape[0]
    n_steps = t // TM
    row = lambda w: pl.BlockSpec((TM, w), lambda i: (i, 0))
    vec = lambda w: _resident((1, w))
    in_specs, args = [row(D_MODEL)], [x]
    if mixer_out is not None:
        o_sb, o_dl, g_sb, g_dl, w_out = mixer_out
        in_specs += [row(D_GROUP), row(D_GROUP), vec(D_GROUP), vec(D_GROUP), _resident((D_MODEL, D_MODEL))]
        args += [o_sb, o_dl, g_sb.reshape(1, D_GROUP), g_dl.reshape(1, D_GROUP), w_out]
    in_specs += [vec(D_MODEL), _resident((D_MODEL, D_FF)), _resident((D_MODEL, D_FF)), _resident((D_FF, D_MODEL))]
    args += [gain.reshape(1, D_MODEL), wg, wu, wd]
    if final_gain is not None:
        in_specs.append(vec(D_MODEL))
        args.append(final_gain.reshape(1, D_MODEL))
    out_specs, out_shape = [row(D_MODEL)], [jax.ShapeDtypeStruct((t, D_MODEL), F32)]
    layer, carried = round_later if round_later is not None else (0, ())
    for w in carried:
        _, rows, cols = w.shape
        hold = next(k for k in range(1, n_steps + 1) if n_steps % k == 0 and rows % (n_steps // k * 16) == 0)
        blk = rows // (n_steps // hold)
        in_specs.append(pl.BlockSpec((None, blk, cols), lambda i, hold=hold: (layer, i // hold, 0)))
        out_specs.append(pl.BlockSpec((blk, cols), lambda i, hold=hold: (i // hold, 0)))
        out_shape.append(jax.ShapeDtypeStruct((rows, cols), BF16))
        args.append(w)
    out = pl.pallas_call(
        functools.partial(_ffn_kernel, mixer_out=mixer_out is not None, final_norm=final_gain is not None,
                          n_carried=len(carried)),
        grid=(n_steps,),
        in_specs=in_specs,
        out_specs=out_specs,
        out_shape=out_shape,
        compiler_params=_params("arbitrary" if carried else "parallel"),
        name="ffn_block",
    )(*args)
    return out if carried else out[0]


def _in_proj_kernel(x_ref, gain_ref, w_ref, cos_ref, sin_ref,
                    qsb_ref, ktsb_ref, vsb_ref, qdl_ref, kdl_ref, vdl_ref):
    h = (_rms(x_ref[...]) * gain_ref[...]).astype(BF16)

    def proj(j):
        return jnp.dot(h, w_ref[:, j * D_GROUP:(j + 1) * D_GROUP], preferred_element_type=F32)

    cos = cos_ref[...]
    sin = sin_ref[...]
    lane = lax.broadcasted_iota(jnp.int32, cos.shape, 1)
    low_half = (lane % HEAD_DIM) < HEAD_DIM // 2

    def rope(t):
        outs = []
        for j in range(D_GROUP // LANES):
            tj = t[:, j * LANES:(j + 1) * LANES]
            ahead = pltpu.roll(tj, LANES - HEAD_DIM // 2, 1)
            behind = pltpu.roll(tj, HEAD_DIM // 2, 1)
            outs.append(tj * cos + jnp.where(low_half, ahead, behind) * sin)
        return jnp.concatenate(outs, axis=1)

    qsb_ref[...] = (proj(0) * Q_SCALE).astype(BF16)
    kt = proj(1).T
    for hp in range(D_GROUP // LANES):
        for kb in range(PROJ_TM // SB_KB):
            ktsb_ref[hp, kb] = kt[hp * LANES:(hp + 1) * LANES, kb * SB_KB:(kb + 1) * SB_KB].astype(BF16)
    vsb_ref[...] = proj(2).astype(BF16)
    qdl_ref[...] = (rope(proj(3)) * Q_SCALE).astype(BF16).astype(F32)
    kdl_ref[...] = rope(proj(4)).astype(BF16).astype(F32)
    vdl_ref[...] = proj(5).astype(BF16).astype(F32)


def _in_proj(x, gain, w_in, cos, sin, batch, seq):
    t = x.shape[0]
    n_seq_tiles = seq // PROJ_TM
    row = lambda w: pl.BlockSpec((PROJ_TM, w), lambda b, i: (b * n_seq_tiles + i, 0))
    grp = jax.ShapeDtypeStruct((t, D_GROUP), BF16)
    grp32 = jax.ShapeDtypeStruct((t, D_GROUP), F32)
    kt_shape = (batch, D_GROUP // LANES, seq // SB_KB, LANES, SB_KB)
    return pl.pallas_call(
        _in_proj_kernel,
        grid=(batch, n_seq_tiles),
        in_specs=[row(D_MODEL), _resident((1, D_MODEL)), _resident((D_MODEL, D_IN)),
                  pl.BlockSpec((PROJ_TM, LANES), lambda b, i: (i, 0)),
                  pl.BlockSpec((PROJ_TM, LANES), lambda b, i: (i, 0))],
        out_specs=[row(D_GROUP),
                   pl.BlockSpec((None, D_GROUP // LANES, PROJ_TM // SB_KB, LANES, SB_KB),
                                lambda b, i: (b, 0, i, 0, 0)),
                   row(D_GROUP), row(D_GROUP), row(D_GROUP), row(D_GROUP)],
        out_shape=[grp, jax.ShapeDtypeStruct(kt_shape, BF16), grp, grp32, grp32, grp32],
        compiler_params=_params("parallel", "parallel"),
        name="in_proj",
    )(x, gain.reshape(1, D_MODEL), w_in, cos, sin)


def _sb_kernel(q_ref, kt_ref, v_ref, csum_ref, o_ref, qs_ref, acc_ref, run_ref):
    i = pl.program_id(2)
    n_heads = LANES // HEAD_DIM
    lane = lax.broadcasted_iota(jnp.int32, (SB_TQ, LANES), 1)
    for hp in range(SB_PAIRS):
        q = q_ref[:, hp * LANES:(hp + 1) * LANES]
        for hh in range(n_heads):
            in_head = (lane >= hh * HEAD_DIM) & (lane < (hh + 1) * HEAD_DIM)
            qs_ref[hp, hh * SB_TQ:(hh + 1) * SB_TQ, :] = jnp.where(in_head, q, jnp.zeros_like(q))
    run_ref[...] = jnp.zeros_like(run_ref)
    acc_ref[...] = jnp.zeros_like(acc_ref)
    csum_w = csum_ref[...]

    def softplus2(kb, mask):
        out = []
        for hp in range(SB_PAIRS):
            w = jnp.dot(qs_ref[hp], kt_ref[hp, kb], preferred_element_type=F32)
            p = jnp.maximum(w, 0.0) + jnp.log(1.0 + jnp.exp2(-jnp.abs(w))) * INV_LN2
            out.append((w, p if mask is None else jnp.where(mask, p, 0.0)))
        return out

    def neg_cumsums(parts):
        stacked = jnp.concatenate([p for _, p in parts], axis=0).astype(BF16)
        out = jnp.dot(stacked, csum_w, preferred_element_type=F32)
        m = parts[0][1].shape[0]
        return [out[n * m:(n + 1) * m] for n in range(len(parts))]

    def finish(kb, parts, neg_cs, mask):
        for hp, ((w, p), neg_c) in enumerate(zip(parts, neg_cs)):
            run = run_ref[hp]
            a = jnp.exp2((w - p) + neg_c + jnp.concatenate([run] * (SB_KB // LANES), axis=1))
            if mask is not None:
                a = jnp.where(mask, a, 0.0)
            run_ref[hp] = run + jnp.broadcast_to(neg_c[:, 0:1] - p[:, 0:1], run.shape)
            v = v_ref[pl.ds(pl.multiple_of(kb * SB_KB, SB_KB), SB_KB), hp * LANES:(hp + 1) * LANES]
            acc_ref[hp] += jnp.dot(a.astype(BF16), v, preferred_element_type=F32)

    def block(kb, mask):
        parts = softplus2(kb, mask)
        finish(kb, parts, neg_cumsums(parts), mask)

    row = lax.broadcasted_iota(jnp.int32, (n_heads * SB_TQ, SB_KB), 0) % SB_TQ
    col = lax.broadcasted_iota(jnp.int32, (n_heads * SB_TQ, SB_KB), 1)
    strictly_before = col < row
    previous = jnp.maximum(i - 1, 0)
    diag, prev = softplus2(i, strictly_before), softplus2(previous, None)
    neg_cs = neg_cumsums(diag + prev)
    finish(i, diag, neg_cs[:SB_PAIRS], strictly_before)
    run_ref[...] = jnp.where(i > 0, run_ref[...], 2 * SB_DEAD_LOG2)
    finish(previous, prev, neg_cs[SB_PAIRS:], None)

    def alive():
        return (jnp.max(run_ref[...]) > SB_DEAD_LOG2).astype(jnp.int32)

    def older(carry):
        kb, _ = carry
        block(kb, None)
        return kb - 1, alive()

    lax.while_loop(lambda c: (c[0] >= 0) & (c[1] > 0), older, (i - 2, alive()))
    for hp in range(SB_PAIRS):
        o_ref[:, hp * LANES:(hp + 1) * LANES] = jnp.where(lane < HEAD_DIM, acc_ref[hp, :SB_TQ, :], acc_ref[hp, SB_TQ:, :])


def _csum_weights():
    j = jnp.arange(SB_KB)[:, None]
    s = jnp.arange(SB_KB)[None, :]
    return jnp.where(j > s, -1.0, 0.0).astype(BF16)


def _sb_attention(q, kt, v, batch, seq):
    t = q.shape[0]
    n_q = seq // SB_TQ
    n_groups = D_GROUP // (SB_PAIRS * LANES)
    m = (LANES // HEAD_DIM) * SB_TQ
    return pl.pallas_call(
        _sb_kernel,
        grid=(batch, n_groups, n_q),
        in_specs=[pl.BlockSpec((SB_TQ, SB_PAIRS * LANES), lambda b, g, i: (b * n_q + i, g)),
                  pl.BlockSpec((None, SB_PAIRS, seq // SB_KB, LANES, SB_KB), lambda b, g, i: (b, g, 0, 0, 0)),
                  pl.BlockSpec((seq, SB_PAIRS * LANES), lambda b, g, i: (b, g)),
                  _resident((SB_KB, SB_KB))],
        out_specs=pl.BlockSpec((SB_TQ, SB_PAIRS * LANES), lambda b, g, i: (b * n_q + i, g)),
        out_shape=jax.ShapeDtypeStruct((t, D_GROUP), F32),
        scratch_shapes=[pltpu.VMEM((SB_PAIRS, m, LANES), BF16),
                        pltpu.VMEM((SB_PAIRS, m, LANES), F32),
                        pltpu.VMEM((SB_PAIRS, m, LANES), F32)],
        compiler_params=_params("parallel", "parallel", "arbitrary"),
        name="sb_attention",
    )(q, kt, v, _csum_weights())


def _dilated_kernel(q_ref, kp_ref, kc_ref, vp_ref, vc_ref, o_ref,
                    q4_ref, k4_ref, v4_ref,
                    on4_ref, mn4_ref, ln4_ref, on16_ref, mn16_ref, ln16_ref, tn_ref, tm_ref, tl_ref):
    st = pl.program_id(2)
    n_heads = LANES // HEAD_DIM
    lane = lax.broadcasted_iota(jnp.int32, (DL_TQ, LANES), 1)
    row = lax.broadcasted_iota(jnp.int32, (DL_TQ, 2 * DL_TQ), 0)
    col = lax.broadcasted_iota(jnp.int32, (DL_TQ, 2 * DL_TQ), 1)
    band = (col >= row) & (col <= row + DL_TQ)
    band_edge = band & (col >= jnp.where(st == 0, DL_TQ, 0))
    head_lanes = [(lane >= hh * HEAD_DIM) & (lane < (hh + 1) * HEAD_DIM) for hh in range(n_heads)]

    def attend(q, k, v, valid):
        qs = jnp.concatenate([jnp.where(head_lanes[hh], q, jnp.zeros_like(q)) for hh in range(n_heads)], axis=0)
        z = lax.dot_general(qs, k, (((1,), (1,)), ((), ())), preferred_element_type=F32)
        z = jnp.where(jnp.concatenate([valid] * n_heads, axis=0), z, -jnp.inf)
        m = jnp.max(z, axis=-1, keepdims=True)
        p = jnp.exp2(z - m)
        den = jnp.sum(p, axis=-1, keepdims=True)
        num = jnp.dot(p.astype(BF16), v, preferred_element_type=F32)
        m, den = jnp.broadcast_to(m, num.shape), jnp.broadcast_to(den, num.shape)
        return tuple(jnp.where(lane < HEAD_DIM, t[:DL_TQ], t[DL_TQ:]) for t in (num, m, den))

    for srcs, dst in (((q_ref,), q4_ref), ((kp_ref, kc_ref), k4_ref), ((vp_ref, vc_ref), v4_ref)):
        per_src = DL_SPAN // 4
        for c in range(4):
            for n, src in enumerate(srcs):
                r0 = (c * len(srcs) + n) * per_src
                dst[r0:r0 + per_src, :] = src[pl.ds(c, per_src, stride=4), :]

    def strided_pattern(d, on_ref, mn_ref, ln_ref):
        sub = d // 4
        q_per_class, k_per_class = DL_SPAN // 4, 2 * DL_SPAN // 4
        for c in range(d):
            c4, r = c % 4, c // 4
            for j in range(DL_SPAN // (d * DL_TQ)):
                q0 = c + d * DL_TQ * j
                qi = r + sub * DL_TQ * j
                ki = k_per_class // 2 + qi - sub * DL_TQ
                q = q4_ref[pl.ds(c4 * q_per_class + qi, DL_TQ, stride=sub), :].astype(BF16)
                k = k4_ref[pl.ds(c4 * k_per_class + ki, 2 * DL_TQ, stride=sub), :].astype(BF16)
                v = v4_ref[pl.ds(c4 * k_per_class + ki, 2 * DL_TQ, stride=sub), :].astype(BF16)
                num, m, den = attend(q, k, v, band_edge if j == 0 else band)
                if sub == 1:
                    rows = pl.ds(q0, DL_TQ, stride=d)
                    on_ref[rows, :], mn_ref[rows, :], ln_ref[rows, :] = num, m, den
                else:
                    rows = pl.ds(c4 * q_per_class + qi, DL_TQ, stride=sub)
                    tn_ref[rows, :], tm_ref[rows, :], tl_ref[rows, :] = num, m, den
        if sub > 1:
            for src, dst in ((tn_ref, on_ref), (tm_ref, mn_ref), (tl_ref, ln_ref)):
                for c in range(4):
                    dst[pl.ds(c, q_per_class, stride=4), :] = src[c * q_per_class:(c + 1) * q_per_class, :]

    strided_pattern(16, on16_ref, mn16_ref, ln16_ref)
    strided_pattern(4, on4_ref, mn4_ref, ln4_ref)

    def mix(part1, r0):
        rows = pl.ds(r0, DL_TQ)
        parts = (part1, (on4_ref[rows, :], mn4_ref[rows, :], ln4_ref[rows, :]),
                 (on16_ref[rows, :], mn16_ref[rows, :], ln16_ref[rows, :]))
        top = functools.reduce(jnp.maximum, [m for _, m, _ in parts])
        scale = [jnp.exp2(m - top) for _, m, _ in parts]
        num = sum(s * n for s, (n, _, _) in zip(scale, parts))
        den = sum(s * d for s, (_, _, d) in zip(scale, parts))
        return num / den

    k_edge = jnp.concatenate([kp_ref[DL_SPAN - DL_TQ:, :], kc_ref[:DL_TQ, :]], axis=0).astype(BF16)
    v_edge = jnp.concatenate([vp_ref[DL_SPAN - DL_TQ:, :], vc_ref[:DL_TQ, :]], axis=0).astype(BF16)
    o_ref[:DL_TQ, :] = mix(attend(q_ref[:DL_TQ, :].astype(BF16), k_edge, v_edge, band_edge), 0)

    for j in range(1, DL_SPAN // DL_TQ):
        q0, k0 = j * DL_TQ, (j - 1) * DL_TQ
        part = attend(q_ref[q0:q0 + DL_TQ, :].astype(BF16), kc_ref[k0:k0 + 2 * DL_TQ, :].astype(BF16),
                      vc_ref[k0:k0 + 2 * DL_TQ, :].astype(BF16), band)
        o_ref[q0:q0 + DL_TQ, :] = mix(part, q0)


def _dilated_attention(q, k, v, batch, seq):
    t = q.shape[0]
    assert seq % DL_SPAN == 0
    n_span = seq // DL_SPAN
    cur = pl.BlockSpec((DL_SPAN, LANES), lambda b, hp, st: (b * n_span + st, hp))
    prev = pl.BlockSpec((DL_SPAN, LANES), lambda b, hp, st: (b * n_span + jnp.maximum(st - 1, 0), hp))
    nat = pltpu.VMEM((DL_SPAN, LANES), F32)
    both = pltpu.VMEM((2 * DL_SPAN, LANES), F32)
    return pl.pallas_call(
        _dilated_kernel,
        grid=(batch, D_GROUP // LANES, n_span),
        in_specs=[cur, prev, cur, prev, cur],
        out_specs=cur,
        out_shape=jax.ShapeDtypeStruct((t, D_GROUP), F32),
        scratch_shapes=[nat, both, both, nat, nat, nat, nat, nat, nat, nat, nat, nat],
        compiler_params=_params("parallel", "parallel", "arbitrary"),
        name="dilated_mixture",
    )(q, k, k, v, v)


def _cast_kernel(*refs):
    n = len(refs) // 2
    for src, dst in zip(refs[:n], refs[n:]):
        dst[...] = src[...].astype(dst.dtype)


def _layer_weights_bf16(layer, *stacked):
    in_specs, out_specs, out_shape = [], [], []
    for w in stacked:
        _, rows, cols = w.shape
        assert rows % (CAST_STEPS * 16) == 0
        in_specs.append(pl.BlockSpec((None, rows // CAST_STEPS, cols), lambda i: (layer, i, 0)))
        out_specs.append(pl.BlockSpec((rows // CAST_STEPS, cols), lambda i: (i, 0)))
        out_shape.append(jax.ShapeDtypeStruct((rows, cols), BF16))
    return pl.pallas_call(
        _cast_kernel,
        grid=(CAST_STEPS,),
        in_specs=in_specs,
        out_specs=out_specs,
        out_shape=out_shape,
        compiler_params=_params("parallel"),
        name="weights_bf16",
    )(*stacked)


def _rope_tables(seq):
    half = HEAD_DIM // 2
    inv_freq = ROPE_THETA ** (-np.arange(half, dtype=np.float64) / half)
    ang = np.arange(seq, dtype=np.float64)[:, None] * inv_freq[None, :]
    cos, sin = np.cos(ang), np.sin(ang)
    reps = LANES // HEAD_DIM
    cos_t = np.tile(np.concatenate([cos, cos], axis=1), (1, reps))
    sin_t = np.tile(np.concatenate([-sin, sin], axis=1), (1, reps))
    return jnp.asarray(cos_t, F32), jnp.asarray(sin_t, F32)


def kernel(x, ffn1_norm, ffn1_w_gate, ffn1_w_up, ffn1_w_down, mix_norm, w_in, sb_out_norm, dil_out_norm, w_out, ffn2_norm, ffn2_w_gate, ffn2_w_up, ffn2_w_down, final_norm):
    batch, seq, _ = x.shape
    depth = ffn1_norm.shape[0]
    cos, sin = _rope_tables(seq)
    h = x.reshape(batch * seq, D_MODEL)
    for layer in range(depth):
        g1, u1, d1 = _layer_weights_bf16(layer, ffn1_w_gate, ffn1_w_up, ffn1_w_down)
        h, w_qkv, w_o, g2, u2, d2 = _ffn_block(
            h, ffn1_norm[layer], g1, u1, d1,
            round_later=(layer, (w_in, w_out, ffn2_w_gate, ffn2_w_up, ffn2_w_down)))
        q_sb, kt_sb, v_sb, q_dl, k_dl, v_dl = _in_proj(h, mix_norm[layer], w_qkv, cos, sin, batch, seq)
        o_sb = _sb_attention(q_sb, kt_sb, v_sb, batch, seq)
        o_dl = _dilated_attention(q_dl, k_dl, v_dl, batch, seq)
        h = _ffn_block(h, ffn2_norm[layer], g2, u2, d2,
                       mixer_out=(o_sb, o_dl, sb_out_norm[layer], dil_out_norm[layer], w_o),
                       final_gain=final_norm if layer == depth - 1 else None)
    return h.reshape(batch, seq, D_MODEL)
```

```python
import functools
import math

import jax
import jax.numpy as jnp
import numpy as np
from jax import lax
from jax.experimental import pallas as pl
from jax.experimental.pallas import tpu as pltpu

D_MODEL = 1024
HEAD_DIM = 64
D_GROUP = 512
D_IN = 6 * D_GROUP
D_FF = 2816
DILATED_PATTERNS = ((128, 1), (512, 4), (2048, 16))
ROPE_THETA = 10000.0
RMS_EPS = 1e-6
HALF_STEP = 0.5

LANES = 128
LOG2E = math.log2(math.e)
INV_LN2 = 1.0 / math.log(2.0)
Q_SCALE = HEAD_DIM ** -0.5 * LOG2E

TM = 512
PROJ_TM = 1024
FF_CHUNK = 256
SB_TQ = 256
SB_PAIRS = 4
SB_KB = 256
SB_DEAD_LOG2 = -160.0
assert SB_TQ == SB_KB
DL_TQ = 128
DL_SPAN = DL_TQ * max(d for _, d in DILATED_PATTERNS)
assert [d for _, d in DILATED_PATTERNS] == [1, 4, 16] and all(w // d == DL_TQ for w, d in DILATED_PATTERNS)
CAST_STEPS = 8
VMEM_LIMIT = 48 * 1024 * 1024

F32 = jnp.float32
BF16 = jnp.bfloat16


def _rms(x):
    return x * lax.rsqrt(jnp.mean(x * x, axis=-1, keepdims=True) + RMS_EPS)


def _params(*sem):
    return pltpu.CompilerParams(dimension_semantics=sem, vmem_limit_bytes=VMEM_LIMIT)


def _resident(shape):
    return pl.BlockSpec(shape, lambda *_: (0,) * len(shape), pipeline_mode=pl.Buffered(1))


def _ffn_kernel(*refs, mixer_out, final_norm, n_carried):
    refs = list(refs)
    carried_out = [refs.pop() for _ in range(n_carried)][::-1]
    o_ref = refs.pop()
    carried_in = [refs.pop() for _ in range(n_carried)][::-1]
    for src, dst in zip(carried_in, carried_out):
        dst[...] = src[...].astype(dst.dtype)
    x = refs.pop(0)[...]
    if mixer_out:
        osb_ref, odl_ref, gsb_ref, gdl_ref, wout_ref = refs[:5]
        del refs[:5]
        sb = (_rms(osb_ref[...]) * gsb_ref[...]).astype(BF16)
        dl = (_rms(odl_ref[...]) * gdl_ref[...]).astype(BF16)
        x = x + jnp.dot(sb, wout_ref[:D_GROUP, :], preferred_element_type=F32)
        x = x + jnp.dot(dl, wout_ref[D_GROUP:, :], preferred_element_type=F32)
    gain_ref, wg_ref, wu_ref, wd_ref = refs[:4]
    h = (_rms(x) * gain_ref[...]).astype(BF16)
    acc = jnp.zeros(x.shape, F32)
    for c in range(0, D_FF, FF_CHUNK):
        g = jnp.dot(h, wg_ref[:, c:c + FF_CHUNK], preferred_element_type=F32)
        u = jnp.dot(h, wu_ref[:, c:c + FF_CHUNK], preferred_element_type=F32)
        a = (g * jax.nn.sigmoid(g) * u).astype(BF16)
        acc = acc + jnp.dot(a, wd_ref[c:c + FF_CHUNK, :], preferred_element_type=F32)
    y = x + HALF_STEP * acc
    if final_norm:
        y = _rms(y) * refs[4][...]
    o_ref[...] = y


def _ffn_block(x, gain, wg, wu, wd, mixer_out=None, final_gain=None, round_later=None, tm=TM):
    t = x.shape[0]
    n_steps = t // tm
    row = lambda w: pl.BlockSpec((tm, w), lambda i: (i, 0))
    vec = lambda w: _resident((1, w))
    in_specs, args = [row(D_MODEL)], [x]
    if mixer_out is not None:
        o_sb, o_dl, g_sb, g_dl, w_out = mixer_out
        in_specs += [row(D_GROUP), row(D_GROUP), vec(D_GROUP), vec(D_GROUP), _resident((D_MODEL, D_MODEL))]
        args += [o_sb, o_dl, g_sb.reshape(1, D_GROUP), g_dl.reshape(1, D_GROUP), w_out]
    in_specs += [vec(D_MODEL), _resident((D_MODEL, D_FF)), _resident((D_MODEL, D_FF)), _resident((D_FF, D_MODEL))]
    args += [gain.reshape(1, D_MODEL), wg, wu, wd]
    if final_gain is not None:
        in_specs.append(vec(D_MODEL))
        args.append(final_gain.reshape(1, D_MODEL))
    out_specs, out_shape = [row(D_MODEL)], [jax.ShapeDtypeStruct((t, D_MODEL), F32)]
    layer, carried = round_later if round_later is not None else (0, ())
    for w in carried:
        _, rows, cols = w.shape
        hold = next(k for k in range(1, n_steps + 1) if n_steps % k == 0 and rows % (n_steps // k * 16) == 0)
        blk = rows // (n_steps // hold)
        in_specs.append(pl.BlockSpec((None, blk, cols), lambda i, hold=hold: (layer, i // hold, 0)))
        out_specs.append(pl.BlockSpec((blk, cols), lambda i, hold=hold: (i // hold, 0)))
        out_shape.append(jax.ShapeDtypeStruct((rows, cols), BF16))
        args.append(w)
    out = pl.pallas_call(
        functools.partial(_ffn_kernel, mixer_out=mixer_out is not None, final_norm=final_gain is not None,
                          n_carried=len(carried)),
        grid=(n_steps,),
        in_specs=in_specs,
        out_specs=out_specs,
        out_shape=out_shape,
        compiler_params=_params("arbitrary" if carried else "parallel"),
        name="ffn_block",
    )(*args)
    return out if carried else out[0]


def _in_proj_kernel(x_ref, gain_ref, w_ref, cos_ref, sin_ref,
                    qsb_ref, ktsb_ref, vsb_ref, qdl_ref, kdl_ref, vdl_ref):
    h = (_rms(x_ref[...]) * gain_ref[...]).astype(BF16)

    def proj(j):
        return jnp.dot(h, w_ref[:, j * D_GROUP:(j + 1) * D_GROUP], preferred_element_type=F32)

    cos = cos_ref[...]
    sin = sin_ref[...]
    lane = lax.broadcasted_iota(jnp.int32, cos.shape, 1)
    low_half = (lane % HEAD_DIM) < HEAD_DIM // 2

    def rope(t):
        outs = []
        for j in range(D_GROUP // LANES):
            tj = t[:, j * LANES:(j + 1) * LANES]
            ahead = pltpu.roll(tj, LANES - HEAD_DIM // 2, 1)
            behind = pltpu.roll(tj, HEAD_DIM // 2, 1)
            outs.append(tj * cos + jnp.where(low_half, ahead, behind) * sin)
        return jnp.concatenate(outs, axis=1)

    qsb_ref[...] = (proj(0) * Q_SCALE).astype(BF16)
    kt = proj(1).T
    for hp in range(D_GROUP // LANES):
        for kb in range(PROJ_TM // SB_KB):
            ktsb_ref[hp, kb] = kt[hp * LANES:(hp + 1) * LANES, kb * SB_KB:(kb + 1) * SB_KB].astype(BF16)
    vsb_ref[...] = proj(2).astype(BF16)
    qdl_ref[...] = (rope(proj(3)) * Q_SCALE).astype(BF16).astype(F32)
    kdl_ref[...] = rope(proj(4)).astype(BF16).astype(F32)
    vdl_ref[...] = proj(5).astype(BF16).astype(F32)


def _in_proj(x, gain, w_in, cos, sin, batch, seq):
    t = x.shape[0]
    n_seq_tiles = seq // PROJ_TM
    row = lambda w: pl.BlockSpec((PROJ_TM, w), lambda b, i: (b * n_seq_tiles + i, 0))
    grp = jax.ShapeDtypeStruct((t, D_GROUP), BF16)
    grp32 = jax.ShapeDtypeStruct((t, D_GROUP), F32)
    kt_shape = (batch, D_GROUP // LANES, seq // SB_KB, LANES, SB_KB)
    return pl.pallas_call(
        _in_proj_kernel,
        grid=(batch, n_seq_tiles),
        in_specs=[row(D_MODEL), _resident((1, D_MODEL)), _resident((D_MODEL, D_IN)),
                  pl.BlockSpec((PROJ_TM, LANES), lambda b, i: (i, 0)),
                  pl.BlockSpec((PROJ_TM, LANES), lambda b, i: (i, 0))],
        out_specs=[row(D_GROUP),
                   pl.BlockSpec((None, D_GROUP // LANES, PROJ_TM // SB_KB, LANES, SB_KB),
                                lambda b, i: (b, 0, i, 0, 0)),
                   row(D_GROUP), row(D_GROUP), row(D_GROUP), row(D_GROUP)],
        out_shape=[grp, jax.ShapeDtypeStruct(kt_shape, BF16), grp, grp32, grp32, grp32],
        compiler_params=_params("parallel", "parallel"),
        name="in_proj",
    )(x, gain.reshape(1, D_MODEL), w_in, cos, sin)


def _sb_kernel(q_ref, kt_ref, v_ref, csum_ref, o_ref, qs_ref, acc_ref, run_ref):
    i = pl.program_id(2)
    n_heads = LANES // HEAD_DIM
    lane = lax.broadcasted_iota(jnp.int32, (SB_TQ, LANES), 1)
    for hp in range(SB_PAIRS):
        q = q_ref[:, hp * LANES:(hp + 1) * LANES]
        for hh in range(n_heads):
            in_head = (lane >= hh * HEAD_DIM) & (lane < (hh + 1) * HEAD_DIM)
            qs_ref[hp, hh * SB_TQ:(hh + 1) * SB_TQ, :] = jnp.where(in_head, q, jnp.zeros_like(q))
    run_ref[...] = jnp.zeros_like(run_ref)
    acc_ref[...] = jnp.zeros_like(acc_ref)
    csum_w = csum_ref[...]

    def softplus2(kb, mask):
        out = []
        for hp in range(SB_PAIRS):
            w = jnp.dot(qs_ref[hp], kt_ref[hp, kb], preferred_element_type=F32)
            p = jnp.maximum(w, 0.0) + jnp.log(1.0 + jnp.exp2(-jnp.abs(w))) * INV_LN2
            out.append((w, p if mask is None else jnp.where(mask, p, 0.0)))
        return out

    def neg_cumsums(parts):
        stacked = jnp.concatenate([p for _, p in parts], axis=0).astype(BF16)
        out = jnp.dot(stacked, csum_w, preferred_element_type=F32)
        m = parts[0][1].shape[0]
        return [out[n * m:(n + 1) * m] for n in range(len(parts))]

    def finish(kb, parts, neg_cs, mask):
        for hp, ((w, p), neg_c) in enumerate(zip(parts, neg_cs)):
            run = run_ref[hp]
            a = jnp.exp2((w - p) + neg_c + jnp.concatenate([run] * (SB_KB // LANES), axis=1))
            if mask is not None:
                a = jnp.where(mask, a, 0.0)
            run_ref[hp] = run + jnp.broadcast_to(neg_c[:, 0:1] - p[:, 0:1], run.shape)
            v = v_ref[pl.ds(pl.multiple_of(kb * SB_KB, SB_KB), SB_KB), hp * LANES:(hp + 1) * LANES]
            acc_ref[hp] += jnp.dot(a.astype(BF16), v, preferred_element_type=F32)

    def block(kb, mask):
        parts = softplus2(kb, mask)
        finish(kb, parts, neg_cumsums(parts), mask)

    row = lax.broadcasted_iota(jnp.int32, (n_heads * SB_TQ, SB_KB), 0) % SB_TQ
    col = lax.broadcasted_iota(jnp.int32, (n_heads * SB_TQ, SB_KB), 1)
    strictly_before = col < row
    previous = jnp.maximum(i - 1, 0)
    diag, prev = softplus2(i, strictly_before), softplus2(previous, None)
    neg_cs = neg_cumsums(diag + prev)
    finish(i, diag, neg_cs[:SB_PAIRS], strictly_before)
    run_ref[...] = jnp.where(i > 0, run_ref[...], 2 * SB_DEAD_LOG2)
    finish(previous, prev, neg_cs[SB_PAIRS:], None)

    def alive():
        return (jnp.max(run_ref[...]) > SB_DEAD_LOG2).astype(jnp.int32)

    def older(carry):
        kb, _ = carry
        block(kb, None)
        return kb - 1, alive()

    lax.while_loop(lambda c: (c[0] >= 0) & (c[1] > 0), older, (i - 2, alive()))
    for hp in range(SB_PAIRS):
        o_ref[:, hp * LANES:(hp + 1) * LANES] = jnp.where(lane < HEAD_DIM, acc_ref[hp, :SB_TQ, :], acc_ref[hp, SB_TQ:, :])


def _csum_weights():
    j = jnp.arange(SB_KB)[:, None]
    s = jnp.arange(SB_KB)[None, :]
    return jnp.where(j > s, -1.0, 0.0).astype(BF16)


def _sb_attention(q, kt, v, batch, seq):
    t = q.shape[0]
    n_q = seq // SB_TQ
    n_groups = D_GROUP // (SB_PAIRS * LANES)
    m = (LANES // HEAD_DIM) * SB_TQ
    return pl.pallas_call(
        _sb_kernel,
        grid=(batch, n_groups, n_q),
        in_specs=[pl.BlockSpec((SB_TQ, SB_PAIRS * LANES), lambda b, g, i: (b * n_q + i, g)),
                  pl.BlockSpec((None, SB_PAIRS, seq // SB_KB, LANES, SB_KB), lambda b, g, i: (b, g, 0, 0, 0)),
                  pl.BlockSpec((seq, SB_PAIRS * LANES), lambda b, g, i: (b, g)),
                  _resident((SB_KB, SB_KB))],
        out_specs=pl.BlockSpec((SB_TQ, SB_PAIRS * LANES), lambda b, g, i: (b * n_q + i, g)),
        out_shape=jax.ShapeDtypeStruct((t, D_GROUP), F32),
        scratch_shapes=[pltpu.VMEM((SB_PAIRS, m, LANES), BF16),
                        pltpu.VMEM((SB_PAIRS, m, LANES), F32),
                        pltpu.VMEM((SB_PAIRS, m, LANES), F32)],
        compiler_params=_params("parallel", "parallel", "arbitrary"),
        name="sb_attention",
    )(q, kt, v, _csum_weights())


def _dilated_kernel(q_ref, kp_ref, kc_ref, vp_ref, vc_ref, o_ref,
                    q4_ref, k4_ref, v4_ref,
                    on4_ref, mn4_ref, ln4_ref, on16_ref, mn16_ref, ln16_ref, tn_ref, tm_ref, tl_ref):
    st = pl.program_id(2)
    n_heads = LANES // HEAD_DIM
    lane = lax.broadcasted_iota(jnp.int32, (DL_TQ, LANES), 1)
    row = lax.broadcasted_iota(jnp.int32, (DL_TQ, 2 * DL_TQ), 0)
    col = lax.broadcasted_iota(jnp.int32, (DL_TQ, 2 * DL_TQ), 1)
    band = (col >= row) & (col <= row + DL_TQ)
    band_edge = band & (col >= jnp.where(st == 0, DL_TQ, 0))
    head_lanes = [(lane >= hh * HEAD_DIM) & (lane < (hh + 1) * HEAD_DIM) for hh in range(n_heads)]

    def attend(q, k, v, valid):
        qs = jnp.concatenate([jnp.where(head_lanes[hh], q, jnp.zeros_like(q)) for hh in range(n_heads)], axis=0)
        z = lax.dot_general(qs, k, (((1,), (1,)), ((), ())), preferred_element_type=F32)
        z = jnp.where(jnp.concatenate([valid] * n_heads, axis=0), z, -jnp.inf)
        m = jnp.max(z, axis=-1, keepdims=True)
        p = jnp.exp2(z - m)
        den = jnp.sum(p, axis=-1, keepdims=True)
        num = jnp.dot(p.astype(BF16), v, preferred_element_type=F32)
        m, den = jnp.broadcast_to(m, num.shape), jnp.broadcast_to(den, num.shape)
        return tuple(jnp.where(lane < HEAD_DIM, t[:DL_TQ], t[DL_TQ:]) for t in (num, m, den))

    for srcs, dst in (((q_ref,), q4_ref), ((kp_ref, kc_ref), k4_ref), ((vp_ref, vc_ref), v4_ref)):
        per_src = DL_SPAN // 4
        for c in range(4):
            for n, src in enumerate(srcs):
                r0 = (c * len(srcs) + n) * per_src
                dst[r0:r0 + per_src, :] = src[pl.ds(c, per_src, stride=4), :]

    def strided_pattern(d, on_ref, mn_ref, ln_ref):
        sub = d // 4
        q_per_class, k_per_class = DL_SPAN // 4, 2 * DL_SPAN // 4
        for c in range(d):
            c4, r = c % 4, c // 4
            for j in range(DL_SPAN // (d * DL_TQ)):
                q0 = c + d * DL_TQ * j
                qi = r + sub * DL_TQ * j
                ki = k_per_class // 2 + qi - sub * DL_TQ
                q = q4_ref[pl.ds(c4 * q_per_class + qi, DL_TQ, stride=sub), :].astype(BF16)
                k = k4_ref[pl.ds(c4 * k_per_class + ki, 2 * DL_TQ, stride=sub), :].astype(BF16)
                v = v4_ref[pl.ds(c4 * k_per_class + ki, 2 * DL_TQ, stride=sub), :].astype(BF16)
                num, m, den = attend(q, k, v, band_edge if j == 0 else band)
                if sub == 1:
                    rows = pl.ds(q0, DL_TQ, stride=d)
                    on_ref[rows, :], mn_ref[rows, :], ln_ref[rows, :] = num, m, den
                else:
                    rows = pl.ds(c4 * q_per_class + qi, DL_TQ, stride=sub)
                    tn_ref[rows, :], tm_ref[rows, :], tl_ref[rows, :] = num, m, den
        if sub > 1:
            for src, dst in ((tn_ref, on_ref), (tm_ref, mn_ref), (tl_ref, ln_ref)):
                for c in range(4):
                    dst[pl.ds(c, q_per_class, stride=4), :] = src[c * q_per_class:(c + 1) * q_per_class, :]

    strided_pattern(16, on16_ref, mn16_ref, ln16_ref)
    strided_pattern(4, on4_ref, mn4_ref, ln4_ref)

    def mix(part1, r0):
        rows = pl.ds(r0, DL_TQ)
        parts = (part1, (on4_ref[rows, :], mn4_ref[rows, :], ln4_ref[rows, :]),
                 (on16_ref[rows, :], mn16_ref[rows, :], ln16_ref[rows, :]))
        top = functools.reduce(jnp.maximum, [m for _, m, _ in parts])
        scale = [jnp.exp2(m - top) for _, m, _ in parts]
        num = sum(s * n for s, (n, _, _) in zip(scale, parts))
        den = sum(s * d for s, (_, _, d) in zip(scale, parts))
        return num / den

    k_edge = jnp.concatenate([kp_ref[DL_SPAN - DL_TQ:, :], kc_ref[:DL_TQ, :]], axis=0).astype(BF16)
    v_edge = jnp.concatenate([vp_ref[DL_SPAN - DL_TQ:, :], vc_ref[:DL_TQ, :]], axis=0).astype(BF16)
    o_ref[:DL_TQ, :] = mix(attend(q_ref[:DL_TQ, :].astype(BF16), k_edge, v_edge, band_edge), 0)

    for j in range(1, DL_SPAN // DL_TQ):
        q0, k0 = j * DL_TQ, (j - 1) * DL_TQ
        part = attend(q_ref[q0:q0 + DL_TQ, :].astype(BF16), kc_ref[k0:k0 + 2 * DL_TQ, :].astype(BF16),
                      vc_ref[k0:k0 + 2 * DL_TQ, :].astype(BF16), band)
        o_ref[q0:q0 + DL_TQ, :] = mix(part, q0)


def _dilated_attention(q, k, v, batch, seq):
    t = q.shape[0]
    assert seq % DL_SPAN == 0
    n_span = seq // DL_SPAN
    cur = pl.BlockSpec((DL_SPAN, LANES), lambda b, hp, st: (b * n_span + st, hp))
    prev = pl.BlockSpec((DL_SPAN, LANES), lambda b, hp, st: (b * n_span + jnp.maximum(st - 1, 0), hp))
    nat = pltpu.VMEM((DL_SPAN, LANES), F32)
    both = pltpu.VMEM((2 * DL_SPAN, LANES), F32)
    return pl.pallas_call(
        _dilated_kernel,
        grid=(batch, D_GROUP // LANES, n_span),
        in_specs=[cur, prev, cur, prev, cur],
        out_specs=cur,
        out_shape=jax.ShapeDtypeStruct((t, D_GROUP), F32),
        scratch_shapes=[nat, both, both, nat, nat, nat, nat, nat, nat, nat, nat, nat],
        compiler_params=_params("parallel", "parallel", "arbitrary"),
        name="dilated_mixture",
    )(q, k, k, v, v)


def _cast_kernel(*refs):
    n = len(refs) // 2
    for src, dst in zip(refs[:n], refs[n:]):
        dst[...] = src[...].astype(dst.dtype)


def _layer_weights_bf16(layer, *stacked):
    in_specs, out_specs, out_shape = [], [], []
    for w in stacked:
        _, rows, cols = w.shape
        assert rows % (CAST_STEPS * 16) == 0
        in_specs.append(pl.BlockSpec((None, rows // CAST_STEPS, cols), lambda i: (layer, i, 0)))
        out_specs.append(pl.BlockSpec((rows // CAST_STEPS, cols), lambda i: (i, 0)))
        out_shape.append(jax.ShapeDtypeStruct((rows, cols), BF16))
    return pl.pallas_call(
        _cast_kernel,
        grid=(CAST_STEPS,),
        in_specs=in_specs,
        out_specs=out_specs,
        out_shape=out_shape,
        compiler_params=_params("parallel"),
        name="weights_bf16",
    )(*stacked)


def _rope_tables(seq):
    half = HEAD_DIM // 2
    inv_freq = ROPE_THETA ** (-np.arange(half, dtype=np.float64) / half)
    ang = np.arange(seq, dtype=np.float64)[:, None] * inv_freq[None, :]
    cos, sin = np.cos(ang), np.sin(ang)
    reps = LANES // HEAD_DIM
    cos_t = np.tile(np.concatenate([cos, cos], axis=1), (1, reps))
    sin_t = np.tile(np.concatenate([-sin, sin], axis=1), (1, reps))
    return jnp.asarray(cos_t, F32), jnp.asarray(sin_t, F32)


def kernel(x, ffn1_norm, ffn1_w_gate, ffn1_w_up, ffn1_w_down, mix_norm, w_in, sb_out_norm, dil_out_norm, w_out, ffn2_norm, ffn2_w_gate, ffn2_w_up, ffn2_w_down, final_norm):
    batch, seq, _ = x.shape
    depth = ffn1_norm.shape[0]
    cos, sin = _rope_tables(seq)
    h = x.reshape(batch * seq, D_MODEL)
    for layer in range(depth):
        g1, u1, d1 = _layer_weights_bf16(layer, ffn1_w_gate, ffn1_w_up, ffn1_w_down)
        h, w_qkv, w_o, g2, u2, d2 = _ffn_block(
            h, ffn1_norm[layer], g1, u1, d1,
            round_later=(layer, (w_in, w_out, ffn2_w_gate, ffn2_w_up, ffn2_w_down)), tm=2 * TM)
        q_sb, kt_sb, v_sb, q_dl, k_dl, v_dl = _in_proj(h, mix_norm[layer], w_qkv, cos, sin, batch, seq)
        o_sb = _sb_attention(q_sb, kt_sb, v_sb, batch, seq)
        o_dl = _dilated_attention(q_dl, k_dl, v_dl, batch, seq)
        h = _ffn_block(h, ffn2_norm[layer], g2, u2, d2,
                       mixer_out=(o_sb, o_dl, sb_out_norm[layer], dil_out_norm[layer], w_o),
                       final_gain=final_norm if layer == depth - 1 else None)
    return h.reshape(batch, seq, D_MODEL)
```
